```python
import jax
import jax.numpy as jnp
from jax import lax
import numpy as np

D_MODEL = 2048
BATCH = 1
SEQ = 8192
DEPTH = 4
DEC_BATCH = 8
DEC_SEQ = 4096
PAST_LEN = 128

HEAD_DIM = 128
A_Q_HEADS = 8
A_KV_HEADS = 2
B_Q_HEADS = 8
B_KV_HEADS = 2
A_Q_W = A_Q_HEADS * HEAD_DIM
A_KV_W = A_KV_HEADS * HEAD_DIM
B_Q_W = B_Q_HEADS * HEAD_DIM
B_KV_W = B_KV_HEADS * HEAD_DIM
IN_W = A_Q_W + 2 * A_KV_W + B_Q_W + 2 * B_KV_W + 2 * D_MODEL
WINDOW = 128
BLOCK = 128
GRID_W = 64
ROPE_THETA = 10000.0
D_FF = ((8 * D_MODEL + 3 * 256 - 1) // (3 * 256)) * 256
N_MOD = 6
EPS = 1e-6
MASK_VALUE = -1e30

kernel_name = "hybrid_gated_window_axial_encoder"


def rms_norm(x, g):
    xf = x.astype(jnp.float32)
    y = xf * lax.rsqrt(jnp.mean(xf * xf, axis=-1, keepdims=True) + EPS)
    return (y * g.astype(jnp.float32)).astype(x.dtype)


def rope_angles(pos, dim):
    inv_freq = ROPE_THETA ** (-jnp.arange(0, dim, 2, dtype=jnp.float32) / dim)
    ang = pos.astype(jnp.float32)[:, None] * inv_freq[None, :]
    return jnp.cos(ang), jnp.sin(ang)


def apply_rotary(x, cos, sin):
    c = cos[None, :, None, :].astype(x.dtype)
    s = sin[None, :, None, :].astype(x.dtype)
    x1, x2 = jnp.split(x, 2, axis=-1)
    return jnp.concatenate([x1 * c - x2 * s, x2 * c + x1 * s], axis=-1)


def apply_axial_rotary(x, cos_r, sin_r, cos_c, sin_c):
    xr, xc = jnp.split(x, 2, axis=-1)
    return jnp.concatenate([apply_rotary(xr, cos_r, sin_r), apply_rotary(xc, cos_c, sin_c)], axis=-1)


def window_attention(q, k, v, sink):
    B, S, Hq, Dh = q.shape
    Hkv = k.shape[2]
    G = Hq // Hkv
    nb = S // BLOCK
    pad = ((0, 0), (BLOCK, BLOCK), (0, 0), (0, 0))
    kp = jnp.pad(k, pad).reshape(B, nb + 2, BLOCK, Hkv, Dh)
    vp = jnp.pad(v, pad).reshape(B, nb + 2, BLOCK, Hkv, Dh)
    kb = jnp.concatenate([kp[:, :-2], kp[:, 1:-1], kp[:, 2:]], axis=2)
    vb = jnp.concatenate([vp[:, :-2], vp[:, 1:-1], vp[:, 2:]], axis=2)
    qb = q.reshape(B, nb, BLOCK, Hkv, G, Dh)
    s = jnp.einsum('bnqhgd,bnkhd->bnhgqk', qb, kb).astype(jnp.float32) * (Dh ** -0.5)
    qpos = jnp.arange(nb)[:, None] * BLOCK + jnp.arange(BLOCK)[None, :]
    kpos = jnp.arange(nb)[:, None] * BLOCK - BLOCK + jnp.arange(3 * BLOCK)[None, :]
    valid = ((jnp.abs(qpos[:, :, None] - kpos[:, None, :]) <= WINDOW)
             & (kpos[:, None, :] >= 0) & (kpos[:, None, :] < S))
    s = jnp.where(valid[None, :, None, None], s, MASK_VALUE)
    sink_l = sink.astype(jnp.float32).reshape(Hkv, G)[None, None, :, :, None, None]
    m = jnp.maximum(jnp.max(s, axis=-1, keepdims=True), sink_l)
    p = jnp.exp(s - m)
    denom = jnp.sum(p, axis=-1, keepdims=True) + jnp.exp(sink_l - m)
    p = (p / denom).astype(v.dtype)
    o = jnp.einsum('bnhgqk,bnkhd->bnqhgd', p, vb)
    return o.reshape(B, S, Hq * Dh)


def dense_attention(q, k, v):
    B, S, Hq, Dh = q.shape
    Hkv = k.shape[2]
    G = Hq // Hkv
    nb = S // BLOCK
    qb = q.reshape(B, nb, BLOCK, Hkv, G, Dh).transpose(1, 0, 2, 3, 4, 5)

    def one_block(qi):
        s = jnp.einsum('bqhgd,bkhd->bhgqk', qi, k).astype(jnp.float32) * (Dh ** -0.5)
        p = jax.nn.softmax(s, axis=-1).astype(v.dtype)
        return jnp.einsum('bhgqk,bkhd->bqhgd', p, v)

    o = lax.map(one_block, qb)
    return o.transpose(1, 0, 2, 3, 4, 5).reshape(B, S, Hq * Dh)


def encoder_layer(x, c, rope1d, rope_axial, g_pre_mix, g_post_mix, g_pre_ffn, g_post_ffn, w_mod, b_mod,
                  w_in, q_norm_b, k_norm_b, sink_a, w_branch_a, w_branch_b, w_out, w_13, w_2):
    B, S, _ = x.shape
    cos1, sin1 = rope1d
    cos_r, sin_r, cos_c, sin_c = rope_axial
    mod = jax.nn.silu(c) @ w_mod + b_mod
    shift_m, scale_m, gate_m, shift_f, scale_f, gate_f = [m[:, None, :] for m in jnp.split(mod, N_MOD, axis=-1)]

    u = rms_norm(x, g_pre_mix) * (1 + scale_m) + shift_m
    z = u @ w_in
    idx = np.cumsum([A_Q_W, A_KV_W, A_KV_W, B_Q_W, B_KV_W, B_KV_W, D_MODEL]).tolist()
    qa, ka, va, qb, kb, vb, ga, gb = jnp.split(z, idx, axis=-1)

    def heads(t, h):
        return t.reshape(B, S, h, HEAD_DIM)

    oa = window_attention(apply_rotary(heads(qa, A_Q_HEADS), cos1, sin1),
                          apply_rotary(heads(ka, A_KV_HEADS), cos1, sin1),
                          heads(va, A_KV_HEADS), sink_a)
    qbh = apply_axial_rotary(rms_norm(heads(qb, B_Q_HEADS), q_norm_b), cos_r, sin_r, cos_c, sin_c)
    kbh = apply_axial_rotary(rms_norm(heads(kb, B_KV_HEADS), k_norm_b), cos_r, sin_r, cos_c, sin_c)
    ob = dense_attention(qbh, kbh, heads(vb, B_KV_HEADS))

    merged = jax.nn.sigmoid(ga) * (oa @ w_branch_a) + jax.nn.sigmoid(gb) * (ob @ w_branch_b)
    y = merged @ w_out
    x = x + gate_m * rms_norm(y, g_post_mix)

    u = rms_norm(x, g_pre_ffn) * (1 + scale_f) + shift_f
    h1, h3 = jnp.split(u @ w_13, 2, axis=-1)
    y = (jax.nn.silu(h1) * h3) @ w_2
    return x + gate_f * rms_norm(y, g_post_ffn)


def encoder(x, c, g_pre_mix, g_post_mix, g_pre_ffn, g_post_ffn, w_mod, b_mod, w_in, q_norm_b, k_norm_b,
            sink_a, w_branch_a, w_branch_b, w_out, w_13, w_2):
    S = x.shape[1]
    rows = S // GRID_W
    t = jnp.arange(S)
    row = jnp.repeat(jnp.arange(rows), GRID_W)
    col = jnp.tile(jnp.arange(GRID_W), rows)
    rope1d = rope_angles(t, HEAD_DIM)
    cos_r, sin_r = rope_angles(row, HEAD_DIM // 2)
    cos_c, sin_c = rope_angles(col, HEAD_DIM // 2)
    rope_axial = (cos_r, sin_r, cos_c, sin_c)
    for l in range(DEPTH):
        x = encoder_layer(x, c, rope1d, rope_axial, g_pre_mix[l], g_post_mix[l], g_pre_ffn[l], g_post_ffn[l],
                          w_mod[l], b_mod[l], w_in[l], q_norm_b[l], k_norm_b[l], sink_a[l],
                          w_branch_a[l], w_branch_b[l], w_out[l], w_13[l], w_2[l])
    return x


def setup_inputs(seed: int = 0) -> dict:
    key = jax.random.key(seed)
    ks = jax.random.split(key, 20)

    def nrm(k, shape, std):
        return jax.random.normal(k, shape, jnp.float32) * std

    def gain(k, shape):
        return 1.0 + 0.05 * jax.random.normal(k, shape, jnp.float32)

    return {
        "x_prompt": nrm(ks[0], (BATCH, SEQ, D_MODEL), 1.0),
        "x_sample": nrm(ks[1], (DEC_BATCH, DEC_SEQ, D_MODEL), 1.0),
        "c_prompt": nrm(ks[2], (BATCH, D_MODEL), 1.0),
        "c_sample": nrm(ks[3], (DEC_BATCH, D_MODEL), 1.0),
        "g_pre_mix": gain(ks[4], (DEPTH, D_MODEL)),
        "g_post_mix": gain(ks[5], (DEPTH, D_MODEL)),
        "g_pre_ffn": gain(ks[6], (DEPTH, D_MODEL)),
        "g_post_ffn": gain(ks[7], (DEPTH, D_MODEL)),
        "w_mod": nrm(ks[8], (DEPTH, D_MODEL, N_MOD * D_MODEL), 0.5 * D_MODEL ** -0.5),
        "b_mod": nrm(ks[9], (DEPTH, N_MOD * D_MODEL), 0.02),
        "w_in": nrm(ks[10], (DEPTH, D_MODEL, IN_W), D_MODEL ** -0.5),
        "q_norm_b": gain(ks[11], (DEPTH, HEAD_DIM)),
        "k_norm_b": gain(ks[12], (DEPTH, HEAD_DIM)),
        "sink_a": nrm(ks[13], (DEPTH, A_Q_HEADS), 0.5),
        "w_branch_a": nrm(ks[14], (DEPTH, A_Q_W, D_MODEL), A_Q_W ** -0.5),
        "w_branch_b": nrm(ks[15], (DEPTH, B_Q_W, D_MODEL), B_Q_W ** -0.5),
        "w_out": nrm(ks[16], (DEPTH, D_MODEL, D_MODEL), D_MODEL ** -0.5),
        "w_13": nrm(ks[17], (DEPTH, D_MODEL, 2 * D_FF), D_MODEL ** -0.5),
        "w_2": nrm(ks[18], (DEPTH, D_FF, D_MODEL), D_FF ** -0.5),
    }


def reference(x_prompt, x_sample, c_prompt, c_sample, g_pre_mix, g_post_mix, g_pre_ffn, g_post_ffn, w_mod, b_mod,
              w_in, q_norm_b, k_norm_b, sink_a, w_branch_a, w_branch_b, w_out, w_13, w_2):
    y_prompt = encoder(x_prompt, c_prompt, g_pre_mix, g_post_mix, g_pre_ffn, g_post_ffn, w_mod, b_mod, w_in,
                       q_norm_b, k_norm_b, sink_a, w_branch_a, w_branch_b, w_out, w_13, w_2)
    y_sample = encoder(x_sample, c_sample, g_pre_mix, g_post_mix, g_pre_ffn, g_post_ffn, w_mod, b_mod, w_in,
                       q_norm_b, k_norm_b, sink_a, w_branch_a, w_branch_b, w_out, w_13, w_2)
    return (y_prompt, y_sample)
```

```python
import functools

import jax
import jax.numpy as jnp
from jax import lax
from jax.experimental import pallas as pl
from jax.experimental.pallas import tpu as pltpu

D_MODEL = 2048
HEAD_DIM = 128
A_Q_HEADS = 8
A_KV_HEADS = 2
B_Q_HEADS = 8
B_KV_HEADS = 2
GROUP = A_Q_HEADS // A_KV_HEADS
WINDOW = 128
GRID_W = 64
ROPE_THETA = 10000.0
N_MOD = 6
EPS = 1e-6
MASK_VALUE = -1e30

QA_OFF = 0
KA_OFF = QA_OFF + A_Q_HEADS * HEAD_DIM
VA_OFF = KA_OFF + A_KV_HEADS * HEAD_DIM
QB_OFF = VA_OFF + A_KV_HEADS * HEAD_DIM
KB_OFF = QB_OFF + B_Q_HEADS * HEAD_DIM
VB_OFF = KB_OFF + B_KV_HEADS * HEAD_DIM
GA_OFF = VB_OFF + B_KV_HEADS * HEAD_DIM
GB_OFF = GA_OFF + D_MODEL
IN_W = GB_OFF + D_MODEL

V7X_VMEM_BYTES = 64 * 1024 * 1024
VMEM_LIMIT = V7X_VMEM_BYTES - 8 * 1024 * 1024

F32 = jnp.float32
BF16 = jnp.bfloat16

MOD_ROWS = 16
MOD_TN = 1024
INPROJ_TM = 512
INPROJ_TN = 1024
WIN_TQ = 512
DENSE_TQ = 128
DENSE_TK = 512
POST_TM = 256
FFN_TM = 512
FFN_TF = 512


def _params(sem):
    return pltpu.CompilerParams(dimension_semantics=sem, vmem_limit_bytes=VMEM_LIMIT)


def _mod_kernel(c_ref, w_ref, b_ref, o_ref):
    c = c_ref[...]
    a = (c * jax.nn.sigmoid(c)).astype(BF16)
    w = w_ref[...].astype(BF16)
    o_ref[...] = jnp.dot(a, w, preferred_element_type=F32) + b_ref[...]


def _modulation(c_all, w_mod, b_mod):
    depth, d, n = w_mod.shape
    return pl.pallas_call(
        _mod_kernel,
        grid=(depth, n // MOD_TN),
        in_specs=[
            pl.BlockSpec((MOD_ROWS, d), lambda l, j: (0, 0)),
            pl.BlockSpec((None, d, MOD_TN), lambda l, j: (l, 0, j)),
            pl.BlockSpec((None, 1, MOD_TN), lambda l, j: (l, 0, j)),
        ],
        out_specs=pl.BlockSpec((None, MOD_ROWS, MOD_TN), lambda l, j: (l, 0, j)),
        out_shape=jax.ShapeDtypeStruct((depth, MOD_ROWS, n), F32),
        compiler_params=_params(("arbitrary", "arbitrary")),
        name="modulation",
    )(c_all, w_mod, b_mod.reshape(depth, 1, n))


def _mod_norm(x, g, scale, shift):
    ms = jnp.mean(x * x, axis=-1, keepdims=True)
    y = x * lax.rsqrt(ms + EPS) * g
    return y * (1.0 + scale) + shift


def _rms(y, g):
    ms = jnp.mean(y * y, axis=-1, keepdims=True)
    return y * lax.rsqrt(ms + EPS) * g


def _rope1d(h, cos, sin_signed):
    return h * cos + pltpu.roll(h, HEAD_DIM // 2, 1) * sin_signed


def _rope_axial(h, cos, sin_lo, sin_hi):
    q = HEAD_DIM // 4
    return h * cos + pltpu.roll(h, HEAD_DIM - q, 1) * sin_lo + pltpu.roll(h, q, 1) * sin_hi


def _inproj_kernel(x_ref, g_ref, sc_ref, sh_ref, w_ref, cos1_ref, sin1_ref, cosx_ref, sinlo_ref,
                   sinhi_ref, qn_ref, kn_ref, z_ref, u_scr, acc_scr):
    j = pl.program_id(1)

    @pl.when(j == 0)
    def _():
        u_scr[...] = _mod_norm(x_ref[...], g_ref[...], sc_ref[...], sh_ref[...]).astype(BF16)

    acc_scr[...] = jnp.dot(u_scr[...], w_ref[...], preferred_element_type=F32)
    qscale = HEAD_DIM ** -0.5

    def head(h):
        return acc_scr[:, h * HEAD_DIM:(h + 1) * HEAD_DIM]

    def put(h, val):
        z_ref[:, h * HEAD_DIM:(h + 1) * HEAD_DIM] = val.astype(BF16)

    def rope_a(h, scale):
        r = _rope1d(head(h), cos1_ref[...], sin1_ref[...])
        put(h, r * scale if scale != 1.0 else r)

    def rope_b(h, gain):
        v = head(h)
        ms = jnp.mean(v * v, axis=-1, keepdims=True)
        vn = v * lax.rsqrt(ms + EPS) * gain
        put(h, _rope_axial(vn, cosx_ref[...], sinlo_ref[...], sinhi_ref[...]))

    heads_per_block = INPROJ_TN // HEAD_DIM

    @pl.when(j == 0)
    def _():
        for h in range(heads_per_block):
            rope_a(h, qscale)

    @pl.when(j == 1)
    def _():
        for h in range(0, 2):
            rope_a(h, 1.0)
        for h in range(2, 4):
            put(h, head(h))
        for h in range(4, 8):
            rope_b(h, qn_ref[...] * qscale)

    @pl.when(j == 2)
    def _():
        for h in range(0, 4):
            rope_b(h, qn_ref[...] * qscale)
        for h in range(4, 6):
            rope_b(h, kn_ref[...])
        for h in range(6, 8):
            put(h, head(h))

    @pl.when(j >= 3)
    def _():
        z_ref[...] = jax.nn.sigmoid(acc_scr[...]).astype(BF16)


def _inproj(x2d, mod_scale, mod_shift, g, w_in, tables, qn, kn, seq):
    t, d = x2d.shape
    tm, tn = INPROJ_TM, INPROJ_TN
    nsb = seq // tm
    row = lambda i, j: (i, 0)
    per_batch = lambda i, j: (i // nsb, 0, 0)
    const2 = lambda i, j: (0, 0)
    pos = lambda i, j: (i % nsb, 0)
    tab_spec = pl.BlockSpec((tm, HEAD_DIM), pos)
    return pl.pallas_call(
        _inproj_kernel,
        grid=(t // tm, IN_W // tn),
        in_specs=[
            pl.BlockSpec((tm, d), row),
            pl.BlockSpec((1, d), const2),
            pl.BlockSpec((None, 1, d), per_batch),
            pl.BlockSpec((None, 1, d), per_batch),
            pl.BlockSpec((d, tn), lambda i, j: (0, j)),
            tab_spec, tab_spec, tab_spec, tab_spec, tab_spec,
            pl.BlockSpec((1, HEAD_DIM), const2),
            pl.BlockSpec((1, HEAD_DIM), const2),
        ],
        out_specs=pl.BlockSpec((tm, tn), lambda i, j: (i, j)),
        out_shape=jax.ShapeDtypeStruct((t, IN_W), BF16),
        scratch_shapes=[pltpu.VMEM((tm, d), BF16), pltpu.VMEM((tm, tn), F32)],
        compiler_params=_params(("arbitrary", "arbitrary")),
        name="inproj",
    )(x2d, g, mod_scale, mod_shift, w_in, *tables, qn, kn)


def _window_kernel(q_ref, kp_ref, km_ref, kn_ref, vp_ref, vm_ref, vn_ref, sink_ref, o_ref, *, seq):
    qi = pl.program_id(2)
    tq = q_ref.shape[0]
    kcat = jnp.concatenate([kp_ref[...], km_ref[...], kn_ref[...]], axis=0)
    vcat = jnp.concatenate([vp_ref[...], vm_ref[...], vn_ref[...]], axis=0)
    sink = sink_ref[:, 0:1]
    rows = GROUP * WINDOW
    r = lax.broadcasted_iota(jnp.int32, (rows, 3 * WINDOW), 0) & (WINDOW - 1)
    c = lax.broadcasted_iota(jnp.int32, (rows, 3 * WINDOW), 1)
    band = (c >= r) & (c <= r + 2 * WINDOW)
    for sb in range(tq // WINDOW):
        base = qi * tq + (sb - 1) * WINDOW
        valid = band & (c >= -base) & (c < seq - base)
        qb = q_ref[sb * WINDOW:(sb + 1) * WINDOW, :]
        qs = jnp.concatenate([qb[:, g * HEAD_DIM:(g + 1) * HEAD_DIM] for g in range(GROUP)], axis=0)
        kj = kcat[sb * WINDOW:(sb + 3) * WINDOW]
        vj = vcat[sb * WINDOW:(sb + 3) * WINDOW]
        s = lax.dot_general(qs, kj, (((1,), (1,)), ((), ())), preferred_element_type=F32)
        s = jnp.where(valid, s, MASK_VALUE)
        m = jnp.maximum(jnp.max(s, axis=-1, keepdims=True), sink)
        p = jnp.exp(s - m)
        denom = jnp.sum(p, axis=-1, keepdims=True) + jnp.exp(sink - m)
        o = jnp.dot(p.astype(BF16), vj, preferred_element_type=F32) / denom
        for g in range(GROUP):
            o_ref[sb * WINDOW:(sb + 1) * WINDOW, g * HEAD_DIM:(g + 1) * HEAD_DIM] = (
                o[g * WINDOW:(g + 1) * WINDOW].astype(BF16))


def _window_attention(z3, sink_rows):
    b, seq, _ = z3.shape
    tq = WIN_TQ
    r = tq // WINDOW
    nblk = seq // WINDOW
    gw = GROUP * HEAD_DIM
    k0 = KA_OFF // HEAD_DIM
    v0 = VA_OFF // HEAD_DIM
    prev = lambda off: (lambda bi, h, qi: (bi, jnp.maximum(qi * r - 1, 0), off + h))
    main = lambda off: (lambda bi, h, qi: (bi, qi, off + h))
    nxt = lambda off: (lambda bi, h, qi: (bi, jnp.minimum((qi + 1) * r, nblk - 1), off + h))
    edge = lambda f: pl.BlockSpec((None, WINDOW, HEAD_DIM), f)
    body = lambda f: pl.BlockSpec((None, tq, HEAD_DIM), f)
    return pl.pallas_call(
        functools.partial(_window_kernel, seq=seq),
        grid=(b, A_KV_HEADS, seq // tq),
        in_specs=[
            pl.BlockSpec((None, tq, gw), lambda bi, h, qi: (bi, qi, QA_OFF // gw + h)),
            edge(prev(k0)), body(main(k0)), edge(nxt(k0)),
            edge(prev(v0)), body(main(v0)), edge(nxt(v0)),
            pl.BlockSpec((None, GROUP * WINDOW, HEAD_DIM), lambda bi, h, qi: (h, 0, 0)),
        ],
        out_specs=pl.BlockSpec((None, tq, gw), lambda bi, h, qi: (bi, qi, h)),
        out_shape=jax.ShapeDtypeStruct((b, seq, A_Q_HEADS * HEAD_DIM), BF16),
        compiler_params=_params(("arbitrary", "arbitrary", "arbitrary")),
        name="window_attn",
    )(z3, z3, z3, z3, z3, z3, z3, sink_rows)


def _dense_kernel(q_ref, k_ref, v_ref, o_ref):
    tq = q_ref.shape[0]
    seq = k_ref.shape[0]
    q = jnp.concatenate([q_ref[:, g * HEAD_DIM:(g + 1) * HEAD_DIM] for g in range(GROUP)], axis=0)
    rows = GROUP * tq

    def body(ci, carry):
        m, l, acc = carry
        start = pl.multiple_of(ci * DENSE_TK, DENSE_TK)
        k = k_ref[pl.ds(start, DENSE_TK), :]
        v = v_ref[pl.ds(start, DENSE_TK), :]
        s = lax.dot_general(q, k, (((1,), (1,)), ((), ())), preferred_element_type=F32)
        m_new = jnp.maximum(m, jnp.max(s, axis=-1, keepdims=True))
        alpha = jnp.exp(m - m_new)
        p = jnp.exp(s - m_new)
        l = alpha * l + jnp.sum(p, axis=-1, keepdims=True)
        acc = alpha * acc + jnp.dot(p.astype(BF16), v, preferred_element_type=F32)
        return m_new, l, acc

    init = (jnp.full((rows, 1), -jnp.inf, F32), jnp.zeros((rows, 1), F32),
            jnp.zeros((rows, HEAD_DIM), F32))
    _, l, acc = lax.fori_loop(0, seq // DENSE_TK, body, init)
    o = acc / l
    for g in range(GROUP):
        o_ref[:, g * HEAD_DIM:(g + 1) * HEAD_DIM] = o[g * tq:(g + 1) * tq].astype(BF16)


def _dense_attention(z3):
    b, seq, _ = z3.shape
    tq = DENSE_TQ
    gw = GROUP * HEAD_DIM
    return pl.pallas_call(
        _dense_kernel,
        grid=(b, B_KV_HEADS, seq // tq),
        in_specs=[
            pl.BlockSpec((None, tq, gw), lambda bi, h, qi: (bi, qi, QB_OFF // gw + h)),
            pl.BlockSpec((None, seq, HEAD_DIM), lambda bi, h, qi: (bi, 0, KB_OFF // HEAD_DIM + h)),
            pl.BlockSpec((None, seq, HEAD_DIM), lambda bi, h, qi: (bi, 0, VB_OFF // HEAD_DIM + h)),
        ],
        out_specs=pl.BlockSpec((None, tq, gw), lambda bi, h, qi: (bi, qi, h)),
        out_shape=jax.ShapeDtypeStruct((b, seq, B_Q_HEADS * HEAD_DIM), BF16),
        compiler_params=_params(("arbitrary", "arbitrary", "arbitrary")),
        name="dense_attn",
    )(z3, z3, z3)


def _postmix_kernel(x_ref, oa_ref, ob_ref, ga0_ref, ga1_ref, gb0_ref, gb1_ref, wa_ref, wb_ref,
                    wo_ref, g_ref, gate_ref, o_ref, m_scr):
    oa = oa_ref[...]
    ob = ob_ref[...]
    half = D_MODEL // 2
    for hi, (ga_ref, gb_ref) in enumerate(((ga0_ref, gb0_ref), (ga1_ref, gb1_ref))):
        cs = slice(hi * half, (hi + 1) * half)
        pa = jnp.dot(oa, wa_ref[:, cs], preferred_element_type=F32)
        pb = jnp.dot(ob, wb_ref[:, cs], preferred_element_type=F32)
        merged = ga_ref[...].astype(F32) * pa + gb_ref[...].astype(F32) * pb
        m_scr[:, cs] = merged.astype(BF16)
    y = jnp.dot(m_scr[...], wo_ref[...], preferred_element_type=F32)
    o_ref[...] = x_ref[...] + gate_ref[...] * _rms(y, g_ref[...])


def _postmix(x2d, oa2d, ob2d, z2d, wa, wb, wo, g, gate, seq):
    t, d = x2d.shape
    tm = POST_TM
    nsb = seq // tm
    half = d // 2
    row = lambda i: (i, 0)
    const2 = lambda i: (0, 0)
    resident = lambda shape: pl.BlockSpec(shape, const2, pipeline_mode=pl.Buffered(1))
    gate_blk = lambda off: pl.BlockSpec((tm, half), lambda i: (i, off // half))
    return pl.pallas_call(
        _postmix_kernel,
        grid=(t // tm,),
        in_specs=[
            pl.BlockSpec((tm, d), row),
            pl.BlockSpec((tm, oa2d.shape[1]), row),
            pl.BlockSpec((tm, ob2d.shape[1]), row),
            gate_blk(GA_OFF), gate_blk(GA_OFF + half), gate_blk(GB_OFF), gate_blk(GB_OFF + half),
            resident(wa.shape), resident(wb.shape), resident(wo.shape),
            pl.BlockSpec((1, d), const2),
            pl.BlockSpec((None, 1, d), lambda i: (i // nsb, 0, 0)),
        ],
        out_specs=pl.BlockSpec((tm, d), row),
        out_shape=jax.ShapeDtypeStruct((t, d), F32),
        scratch_shapes=[pltpu.VMEM((tm, d), BF16)],
        compiler_params=_params(("arbitrary",)),
        name="postmix",
    )(x2d, oa2d, ob2d, z2d, z2d, z2d, z2d, wa, wb, wo, g, gate)


def _ffn_kernel(x_ref, g_ref, sc_ref, sh_ref, w1_ref, w3_ref, w2_ref, gpost_ref, gate_ref, o_ref,
                u_scr):
    k = pl.program_id(1)
    last = pl.num_programs(1) - 1

    @pl.when(k == 0)
    def _():
        u_scr[...] = _mod_norm(x_ref[...], g_ref[...], sc_ref[...], sh_ref[...]).astype(BF16)

    u = u_scr[...]
    h1 = jnp.dot(u, w1_ref[...], preferred_element_type=F32)
    h3 = jnp.dot(u, w3_ref[...], preferred_element_type=F32)
    h = (h1 * jax.nn.sigmoid(h1) * h3).astype(BF16)
    part = jnp.dot(h, w2_ref[...], preferred_element_type=F32)

    @pl.when(k == 0)
    def _():
        o_ref[...] = part

    @pl.when(k > 0)
    def _():
        o_ref[...] += part

    @pl.when(k == last)
    def _():
        o_ref[...] = x_ref[...] + gate_ref[...] * _rms(o_ref[...], gpost_ref[...])


def _ffn(x2d, g_pre, mod_scale, mod_shift, w13, w2, g_post, gate, seq):
    t, d = x2d.shape
    tm, tf = FFN_TM, FFN_TF
    d_ff = w2.shape[0]
    nk = d_ff // tf
    nsb = seq // tm
    row = lambda i, k: (i, 0)
    const2 = lambda i, k: (0, 0)
    per_batch = lambda i, k: (i // nsb, 0, 0)
    return pl.pallas_call(
        _ffn_kernel,
        grid=(t // tm, nk),
        in_specs=[
            pl.BlockSpec((tm, d), row),
            pl.BlockSpec((1, d), const2),
            pl.BlockSpec((None, 1, d), per_batch),
            pl.BlockSpec((None, 1, d), per_batch),
            pl.BlockSpec((d, tf), lambda i, k: (0, k)),
            pl.BlockSpec((d, tf), lambda i, k: (0, nk + k)),
            pl.BlockSpec((tf, d), lambda i, k: (k, 0)),
            pl.BlockSpec((1, d), const2),
            pl.BlockSpec((None, 1, d), per_batch),
        ],
        out_specs=pl.BlockSpec((tm, d), row),
        out_shape=jax.ShapeDtypeStruct((t, d), F32),
        scratch_shapes=[pltpu.VMEM((tm, d), BF16)],
        compiler_params=_params(("arbitrary", "arbitrary")),
        name="ffn",
    )(x2d, g_pre, mod_scale, mod_shift, w13, w13, w2, g_post, gate)


def _rope_tables(seq):
    def angles(pos, dim):
        inv_freq = ROPE_THETA ** (-jnp.arange(0, dim, 2, dtype=F32) / dim)
        ang = pos.astype(F32)[:, None] * inv_freq[None, :]
        return jnp.cos(ang), jnp.sin(ang)

    t = jnp.arange(seq)
    c1, s1 = angles(t, HEAD_DIM)
    cr, sr = angles(t // GRID_W, HEAD_DIM // 2)
    cc, sc = angles(t % GRID_W, HEAD_DIM // 2)
    zero = jnp.zeros_like(sr)
    cos1 = jnp.concatenate([c1, c1], axis=-1)
    sin1 = jnp.concatenate([-s1, s1], axis=-1)
    cosx = jnp.concatenate([cr, cr, cc, cc], axis=-1)
    sin_lo = jnp.concatenate([-sr, zero, -sc, zero], axis=-1)
    sin_hi = jnp.concatenate([zero, sr, zero, sc], axis=-1)
    return cos1, sin1, cosx, sin_lo, sin_hi


def _encoder(x, mods, weights):
    b, seq, d = x.shape
    tables = _rope_tables(seq)
    x2d = x.reshape(b * seq, d)
    for l, wl in enumerate(weights):
        shift_m, scale_m, gate_m, shift_f, scale_f, gate_f = [
            mods[l][:, None, i * d:(i + 1) * d] for i in range(N_MOD)]
        z2d = _inproj(x2d, scale_m, shift_m, wl["g_pre_mix"], wl["w_in"], tables, wl["q_norm_b"],
                      wl["k_norm_b"], seq)
        z3 = z2d.reshape(b, seq, IN_W)
        oa = _window_attention(z3, wl["sink_rows"])
        ob = _dense_attention(z3)
        x2d = _postmix(x2d, oa.reshape(b * seq, -1), ob.reshape(b * seq, -1), z2d, wl["w_branch_a"],
                       wl["w_branch_b"], wl["w_out"], wl["g_post_mix"], gate_m, seq)
        x2d = _ffn(x2d, wl["g_pre_ffn"], scale_f, shift_f, wl["w_13"], wl["w_2"], wl["g_post_ffn"],
                   gate_f, seq)
    return x2d.reshape(b, seq, d)


def kernel(x_prompt, x_sample, c_prompt, c_sample, g_pre_mix, g_post_mix, g_pre_ffn, g_post_ffn, w_mod,
           b_mod, w_in, q_norm_b, k_norm_b, sink_a, w_branch_a, w_branch_b, w_out, w_13, w_2):
    depth = w_in.shape[0]
    nb_p, nb_s = c_prompt.shape[0], c_sample.shape[0]
    assert nb_p + nb_s <= MOD_ROWS
    c_all = jnp.zeros((MOD_ROWS, D_MODEL), F32)
    c_all = c_all.at[:nb_p].set(c_prompt).at[nb_p:nb_p + nb_s].set(c_sample)
    mod = _modulation(c_all, w_mod, b_mod)
    mods_p = [mod[l, :nb_p] for l in range(depth)]
    mods_s = [mod[l, nb_p:nb_p + nb_s] for l in range(depth)]

    weights = []
    for l in range(depth):
        sink_rows = jnp.broadcast_to(
            sink_a[l].astype(F32).reshape(A_KV_HEADS, GROUP, 1, 1),
            (A_KV_HEADS, GROUP, WINDOW, HEAD_DIM)).reshape(A_KV_HEADS, GROUP * WINDOW, HEAD_DIM)
        weights.append({
            "g_pre_mix": g_pre_mix[l][None], "g_post_mix": g_post_mix[l][None],
            "g_pre_ffn": g_pre_ffn[l][None], "g_post_ffn": g_post_ffn[l][None],
            "q_norm_b": q_norm_b[l][None], "k_norm_b": k_norm_b[l][None],
            "sink_rows": sink_rows,
            "w_in": w_in[l].astype(BF16),
            "w_branch_a": w_branch_a[l].astype(BF16), "w_branch_b": w_branch_b[l].astype(BF16),
            "w_out": w_out[l].astype(BF16),
            "w_13": w_13[l].astype(BF16), "w_2": w_2[l].astype(BF16),
        })

    y_prompt = _encoder(x_prompt, mods_p, weights)
    y_sample = _encoder(x_sample, mods_s, weights)
    return (y_prompt, y_sample)
```

```python
import functools

import jax
import jax.numpy as jnp
from jax import lax
from jax.experimental import pallas as pl
from jax.experimental.pallas import tpu as pltpu

D_MODEL = 2048
HEAD_DIM = 128
A_Q_HEADS = 8
A_KV_HEADS = 2
B_Q_HEADS = 8
B_KV_HEADS = 2
GROUP = A_Q_HEADS // A_KV_HEADS
WINDOW = 128
GRID_W = 64
ROPE_THETA = 10000.0
N_MOD = 6
EPS = 1e-6
MASK_VALUE = -1e30
LOG2_E = 1.4426950408889634

QA_OFF = 0
KA_OFF = QA_OFF + A_Q_HEADS * HEAD_DIM
VA_OFF = KA_OFF + A_KV_HEADS * HEAD_DIM
QB_OFF = VA_OFF + A_KV_HEADS * HEAD_DIM
KB_OFF = QB_OFF + B_Q_HEADS * HEAD_DIM
VB_OFF = KB_OFF + B_KV_HEADS * HEAD_DIM
GA_OFF = VB_OFF + B_KV_HEADS * HEAD_DIM
GB_OFF = GA_OFF + D_MODEL
IN_W = GB_OFF + D_MODEL

CHUNK_W = GROUP * HEAD_DIM
Z_CHUNKS = (
    (QA_OFF, "qa"), (GA_OFF, "gate"),
    (QA_OFF + CHUNK_W, "qa"), (GA_OFF + CHUNK_W, "gate"),
    (KA_OFF, "kva"), (GA_OFF + 2 * CHUNK_W, "gate"),
    (QB_OFF, "qb"), (GA_OFF + 3 * CHUNK_W, "gate"),
    (QB_OFF + CHUNK_W, "qb"), (GB_OFF, "gate"),
    (KB_OFF, "kvb"), (GB_OFF + CHUNK_W, "gate"),
    (GB_OFF + 2 * CHUNK_W, "gate"), (GB_OFF + 3 * CHUNK_W, "gate"),
)
assert len(Z_CHUNKS) * CHUNK_W == IN_W


def _z_chunk(src_off):
    return [off for off, _ in Z_CHUNKS].index(src_off)

V7X_VMEM_BYTES = 64 * 1024 * 1024
VMEM_LIMIT = V7X_VMEM_BYTES - 8 * 1024 * 1024

F32 = jnp.float32
BF16 = jnp.bfloat16

MOD_ROWS = 16
MOD_TN = 1024
INPROJ_TM = 512
INPROJ_TN = 1024
WIN_TQ = 512
DENSE_TQ = 256
DENSE_TK = 512
DENSE_UNROLL_SHORT = 2
DENSE_UNROLL_LONG = 4
DENSE_LONG_CHUNKS = 16
POST_TM = 256
FFN_TM = 512
FFN_TF = 512


def _params(sem):
    return pltpu.CompilerParams(dimension_semantics=sem, vmem_limit_bytes=VMEM_LIMIT)


def _mod_kernel(c_ref, w_ref, b_ref, o_ref):
    c = c_ref[...]
    a = (c * jax.nn.sigmoid(c)).astype(BF16)
    w = w_ref[...].astype(BF16)
    o_ref[...] = jnp.dot(a, w, preferred_element_type=F32) + b_ref[...]


def _modulation(c_all, w_mod, b_mod):
    depth, d, n = w_mod.shape
    return pl.pallas_call(
        _mod_kernel,
        grid=(depth, n // MOD_TN),
        in_specs=[
            pl.BlockSpec((MOD_ROWS, d), lambda l, j: (0, 0)),
            pl.BlockSpec((None, d, MOD_TN), lambda l, j: (l, 0, j)),
            pl.BlockSpec((None, 1, MOD_TN), lambda l, j: (l, 0, j)),
        ],
        out_specs=pl.BlockSpec((None, MOD_ROWS, MOD_TN), lambda l, j: (l, 0, j)),
        out_shape=jax.ShapeDtypeStruct((depth, MOD_ROWS, n), F32),
        compiler_params=_params(("arbitrary", "arbitrary")),
        name="modulation",
    )(c_all, w_mod, b_mod.reshape(depth, 1, n))


def _mod_norm(x, g, scale, shift):
    ms = jnp.mean(x * x, axis=-1, keepdims=True)
    y = x * lax.rsqrt(ms + EPS) * g
    return y * (1.0 + scale) + shift


def _sigmoid(x):
    return 0.5 * jnp.tanh(0.5 * x) + 0.5


def _rms(y, g):
    ms = jnp.mean(y * y, axis=-1, keepdims=True)
    return y * lax.rsqrt(ms + EPS) * g


def _rope1d(h, cos, sin_signed):
    return h * cos + pltpu.roll(h, HEAD_DIM // 2, 1) * sin_signed


def _rope_axial(h, cos, sin_lo, sin_hi):
    q = HEAD_DIM // 4
    return h * cos + pltpu.roll(h, HEAD_DIM - q, 1) * sin_lo + pltpu.roll(h, q, 1) * sin_hi


def _inproj_kernel(x_ref, g_ref, sc_ref, sh_ref, w_ref, cos1_ref, sin1_ref, cosx_ref, sinlo_ref,
                   sinhi_ref, qn_ref, kn_ref, z_ref, u_scr):
    j = pl.program_id(1)
    qscale = HEAD_DIM ** -0.5 * LOG2_E

    def rope_a(v, scale):
        r = _rope1d(v, cos1_ref[...], sin1_ref[...])
        return r * scale if scale != 1.0 else r

    def rope_b(v, gain):
        ms = jnp.mean(v * v, axis=-1, keepdims=True)
        vn = v * lax.rsqrt(ms + EPS) * gain
        return _rope_axial(vn, cosx_ref[...], sinlo_ref[...], sinhi_ref[...])

    q_a = lambda v: rope_a(v, qscale)
    k_a = lambda v: rope_a(v, 1.0)
    q_b = lambda v: rope_b(v, qn_ref[...] * qscale)
    k_b = lambda v: rope_b(v, kn_ref[...])
    keep = lambda v: v
    head_ops = {
        "qa": [q_a] * GROUP,
        "kva": [k_a] * A_KV_HEADS + [keep] * A_KV_HEADS,
        "qb": [q_b] * GROUP,
        "kvb": [k_b] * B_KV_HEADS + [keep] * B_KV_HEADS,
    }
    chunks_per_block = INPROJ_TN // CHUNK_W

    def run(kinds):
        for ci, kind in enumerate(kinds):
            cols = slice(ci * CHUNK_W, (ci + 1) * CHUNK_W)
            acc = jnp.dot(u_scr[...], w_ref[:, cols], preferred_element_type=F32)
            if kind == "gate":
                z_ref[:, cols] = _sigmoid(acc).astype(BF16)
                continue
            for hh, op in enumerate(head_ops[kind]):
                lo = ci * CHUNK_W + hh * HEAD_DIM
                z_ref[:, lo:lo + HEAD_DIM] = op(acc[:, hh * HEAD_DIM:(hh + 1) * HEAD_DIM]).astype(BF16)

    n_blocks = len(Z_CHUNKS) // chunks_per_block
    block_kinds = [tuple(k for _, k in Z_CHUNKS[bj * chunks_per_block:(bj + 1) * chunks_per_block])
                   for bj in range(n_blocks)]
    @pl.when(j == 0)
    def _():
        u_scr[...] = _mod_norm(x_ref[...], g_ref[...], sc_ref[...], sh_ref[...]).astype(BF16)
        run(block_kinds[0])

    for kinds in dict.fromkeys(block_kinds[1:]):
        cond = functools.reduce(jnp.logical_or, [j == bj for bj in range(1, n_blocks)
                                                 if block_kinds[bj] == kinds])
        pl.when(cond)(functools.partial(run, kinds))


def _inproj(x2d, mod_scale, mod_shift, g, w_in, tables, qn, kn, seq):
    t, d = x2d.shape
    tm, tn = INPROJ_TM, INPROJ_TN
    nsb = seq // tm
    row = lambda i, j: (i, 0)
    per_batch = lambda i, j: (i // nsb, 0, 0)
    const2 = lambda i, j: (0, 0)
    pos = lambda i, j: (i % nsb, 0)
    tab_spec = pl.BlockSpec((tm, HEAD_DIM), pos)
    return pl.pallas_call(
        _inproj_kernel,
        grid=(t // tm, IN_W // tn),
        in_specs=[
            pl.BlockSpec((tm, d), row),
            pl.BlockSpec((1, d), const2),
            pl.BlockSpec((None, 1, d), per_batch),
            pl.BlockSpec((None, 1, d), per_batch),
            pl.BlockSpec((d, tn), lambda i, j: (0, j)),
            tab_spec, tab_spec, tab_spec, tab_spec, tab_spec,
            pl.BlockSpec((1, HEAD_DIM), const2),
            pl.BlockSpec((1, HEAD_DIM), const2),
        ],
        out_specs=pl.BlockSpec((tm, tn), lambda i, j: (i, j)),
        out_shape=jax.ShapeDtypeStruct((t, IN_W), BF16),
        scratch_shapes=[pltpu.VMEM((tm, d), BF16)],
        compiler_params=_params(("arbitrary", "arbitrary")),
        name="inproj",
    )(x2d, g, mod_scale, mod_shift, w_in, *tables, qn, kn)


def _window_kernel(q_ref, kp_ref, km_ref, kn_ref, vp_ref, vm_ref, vn_ref, sink_ref, o_ref, *, seq):
    qi = pl.program_id(2)
    tq = q_ref.shape[0]
    kcat = jnp.concatenate([kp_ref[...], km_ref[...], kn_ref[...]], axis=0)
    vcat = jnp.concatenate([vp_ref[...], vm_ref[...], vn_ref[...]], axis=0)
    vt = vcat.astype(F32).T.astype(BF16)
    sink = sink_ref[...] * LOG2_E
    lanes = GROUP * WINDOW
    r = lax.broadcasted_iota(jnp.int32, (1, lanes), 1) & (WINDOW - 1)
    kk = lax.broadcasted_iota(jnp.int32, (3 * WINDOW, lanes), 0)
    for sb in range(tq // WINDOW):
        base = qi * tq + (sb - 1) * WINDOW
        lo = jnp.maximum(r, -base)
        hi = jnp.minimum(r + 2 * WINDOW, seq - 1 - base)
        span = lax.bitcast_convert_type(hi - lo, jnp.uint32)
        valid = lax.bitcast_convert_type(kk - lo, jnp.uint32) <= span
        qb = q_ref[sb * WINDOW:(sb + 1) * WINDOW, :]
        qs = jnp.concatenate([qb[:, g * HEAD_DIM:(g + 1) * HEAD_DIM] for g in range(GROUP)], axis=0)
        kj = kcat[sb * WINDOW:(sb + 3) * WINDOW]
        s = lax.dot_general(kj, qs, (((1,), (1,)), ((), ())), preferred_element_type=F32)
        s = jnp.where(valid, s, MASK_VALUE)
        m = jnp.maximum(jnp.max(s, axis=0, keepdims=True), sink)
        p = jnp.exp2(s - m)
        denom = jnp.sum(p, axis=0, keepdims=True) + jnp.exp2(sink - m)
        ot = jnp.dot(vt[:, sb * WINDOW:(sb + 3) * WINDOW], p.astype(BF16),
                     preferred_element_type=F32) / denom
        for g in range(GROUP):
            o_ref[sb * WINDOW:(sb + 1) * WINDOW, g * HEAD_DIM:(g + 1) * HEAD_DIM] = (
                ot[:, g * WINDOW:(g + 1) * WINDOW].T.astype(BF16))


def _window_attention(z3, sink_rows):
    b, seq, _ = z3.shape
    tq = WIN_TQ
    r = tq // WINDOW
    nblk = seq // WINDOW
    gw = CHUNK_W
    heads_per_chunk = CHUNK_W // HEAD_DIM
    k0 = _z_chunk(KA_OFF) * heads_per_chunk
    v0 = k0 + A_KV_HEADS
    q_chunks = [_z_chunk(QA_OFF + g * CHUNK_W) for g in range(A_KV_HEADS)]
    assert q_chunks[1] - q_chunks[0] == 2
    prev = lambda off: (lambda bi, h, qi: (bi, jnp.maximum(qi * r - 1, 0), off + h))
    main = lambda off: (lambda bi, h, qi: (bi, qi, off + h))
    nxt = lambda off: (lambda bi, h, qi: (bi, jnp.minimum((qi + 1) * r, nblk - 1), off + h))
    edge = lambda f: pl.BlockSpec((None, WINDOW, HEAD_DIM), f)
    body = lambda f: pl.BlockSpec((None, tq, HEAD_DIM), f)
    return pl.pallas_call(
        functools.partial(_window_kernel, seq=seq),
        grid=(b, A_KV_HEADS, seq // tq),
        in_specs=[
            pl.BlockSpec((None, tq, gw), lambda bi, h, qi: (bi, qi, q_chunks[0] + 2 * h)),
            edge(prev(k0)), body(main(k0)), edge(nxt(k0)),
            edge(prev(v0)), body(main(v0)), edge(nxt(v0)),
            pl.BlockSpec((None, 1, GROUP * WINDOW), lambda bi, h, qi: (h, 0, 0)),
        ],
        out_specs=pl.BlockSpec((None, tq, gw), lambda bi, h, qi: (bi, qi, h)),
        out_shape=jax.ShapeDtypeStruct((b, seq, A_Q_HEADS * HEAD_DIM), BF16),
        compiler_params=_params(("arbitrary", "arbitrary", "arbitrary")),
        name="window_attn",
    )(z3, z3, z3, z3, z3, z3, z3, sink_rows)


def _dense_kernel(q_ref, k_ref, v_ref, o_ref, vt_scr, s0_scr, s1_scr):
    tq = q_ref.shape[0]
    seq = k_ref.shape[0]
    cols = GROUP * tq

    @pl.when(pl.program_id(2) == 0)
    def _():
        for c in range(seq // DENSE_TK):
            blk = v_ref[c * DENSE_TK:(c + 1) * DENSE_TK, :].astype(F32)
            vt_scr[:, c * DENSE_TK:(c + 1) * DENSE_TK] = blk.T.astype(BF16)

    q = jnp.concatenate([q_ref[:, g * HEAD_DIM:(g + 1) * HEAD_DIM] for g in range(GROUP)], axis=0)

    def scores(ci, s_scr):
        start = pl.multiple_of(ci * DENSE_TK, DENSE_TK)
        k = k_ref[pl.ds(start, DENSE_TK), :]
        s = lax.dot_general(k, q, (((1,), (1,)), ((), ())), preferred_element_type=F32)
        s_scr[...] = s
        return jnp.max(s, axis=0, keepdims=True)

    def update(ci, s_scr, mc, state):
        m, l, acc = state
        start = pl.multiple_of(ci * DENSE_TK, DENSE_TK)
        vt = vt_scr[:, pl.ds(start, DENSE_TK)]
        m_new = jnp.maximum(m, mc)
        alpha = jnp.exp2(m - m_new)
        p = jnp.exp2(s_scr[...] - m_new)
        l = alpha * l + jnp.sum(p, axis=0, keepdims=True)
        acc = alpha * acc + jnp.dot(vt, p.astype(BF16), preferred_element_type=F32)
        return m_new, l, acc

    n_chunks = seq // DENSE_TK
    bufs = (s0_scr, s1_scr)
    unroll = DENSE_UNROLL_LONG if n_chunks >= DENSE_LONG_CHUNKS else DENSE_UNROLL_SHORT
    unroll = min(unroll, n_chunks)
    assert n_chunks % unroll == 0 and (unroll % 2 == 0 or n_chunks == 1)

    def group(base, mc, state, last):
        for j in range(unroll):
            c = base + j
            nxt = None if (last and j == unroll - 1) else scores(c + 1, bufs[(j + 1) % 2])
            state = update(c, bufs[j % 2], mc, state)
            mc = nxt
        return mc, state

    state = (jnp.full((1, cols), -jnp.inf, F32), jnp.zeros((1, cols), F32),
             jnp.zeros((HEAD_DIM, cols), F32))
    mc, state = lax.fori_loop(
        0, n_chunks // unroll - 1,
        lambda i, carry: group(i * unroll, carry[0], carry[1], False),
        (scores(0, bufs[0]), state))
    _, (_, l, acc) = group(n_chunks - unroll, mc, state, True)
    o = acc / l
    for g in range(GROUP):
        o_ref[:, g * HEAD_DIM:(g + 1) * HEAD_DIM] = o[:, g * tq:(g + 1) * tq].T.astype(BF16)


def _dense_attention(z3):
    b, seq, _ = z3.shape
    tq = DENSE_TQ
    gw = CHUNK_W
    heads_per_chunk = CHUNK_W // HEAD_DIM
    k0 = _z_chunk(KB_OFF) * heads_per_chunk
    v0 = k0 + B_KV_HEADS
    q_chunks = [_z_chunk(QB_OFF + g * CHUNK_W) for g in range(B_KV_HEADS)]
    assert q_chunks[1] - q_chunks[0] == 2
    return pl.pallas_call(
        _dense_kernel,
        grid=(b, B_KV_HEADS, seq // tq),
        in_specs=[
            pl.BlockSpec((None, tq, gw), lambda bi, h, qi: (bi, qi, q_chunks[0] + 2 * h)),
            pl.BlockSpec((None, seq, HEAD_DIM), lambda bi, h, qi: (bi, 0, k0 + h)),
            pl.BlockSpec((None, seq, HEAD_DIM), lambda bi, h, qi: (bi, 0, v0 + h)),
        ],
        out_specs=pl.BlockSpec((None, tq, gw), lambda bi, h, qi: (bi, qi, h)),
        out_shape=jax.ShapeDtypeStruct((b, seq, B_Q_HEADS * HEAD_DIM), BF16),
        scratch_shapes=[pltpu.VMEM((HEAD_DIM, seq), BF16),
                        pltpu.VMEM((DENSE_TK, GROUP * tq), F32),
                        pltpu.VMEM((DENSE_TK, GROUP * tq), F32)],
        compiler_params=_params(("arbitrary", "arbitrary", "arbitrary")),
        name="dense_attn",
    )(z3, z3, z3)


def _postmix_kernel(x_ref, oa_ref, ob_ref, *rest):
    n_gate = D_MODEL // CHUNK_W
    ga_refs, gb_refs = rest[:n_gate], rest[n_gate:2 * n_gate]
    wa_ref, wb_ref, wo_ref, g_ref, gate_ref, o_ref, m_scr = rest[2 * n_gate:]
    oa = oa_ref[...]
    ob = ob_ref[...]
    for ci, (ga_ref, gb_ref) in enumerate(zip(ga_refs, gb_refs)):
        cs = slice(ci * CHUNK_W, (ci + 1) * CHUNK_W)
        pa = jnp.dot(oa, wa_ref[:, cs], preferred_element_type=F32)
        pb = jnp.dot(ob, wb_ref[:, cs], preferred_element_type=F32)
        merged = ga_ref[...].astype(F32) * pa + gb_ref[...].astype(F32) * pb
        m_scr[:, cs] = merged.astype(BF16)
    y = jnp.dot(m_scr[...], wo_ref[...], preferred_element_type=F32)
    o_ref[...] = x_ref[...] + gate_ref[...] * _rms(y, g_ref[...])


def _postmix(x2d, oa2d, ob2d, z2d, wa, wb, wo, g, gate, seq):
    t, d = x2d.shape
    tm = POST_TM
    nsb = seq // tm
    n_gate = d // CHUNK_W
    row = lambda i: (i, 0)
    const2 = lambda i: (0, 0)
    resident = lambda shape: pl.BlockSpec(shape, const2, pipeline_mode=pl.Buffered(1))
    gate_blk = lambda off: pl.BlockSpec((tm, CHUNK_W), functools.partial(
        lambda chunk, i: (i, chunk), _z_chunk(off)))
    gate_specs = ([gate_blk(GA_OFF + c * CHUNK_W) for c in range(n_gate)]
                  + [gate_blk(GB_OFF + c * CHUNK_W) for c in range(n_gate)])
    return pl.pallas_call(
        _postmix_kernel,
        grid=(t // tm,),
        in_specs=[
            pl.BlockSpec((tm, d), row),
            pl.BlockSpec((tm, oa2d.shape[1]), row),
            pl.BlockSpec((tm, ob2d.shape[1]), row),
            *gate_specs,
            resident(wa.shape), resident(wb.shape), resident(wo.shape),
            pl.BlockSpec((1, d), const2),
            pl.BlockSpec((None, 1, d), lambda i: (i // nsb, 0, 0)),
        ],
        out_specs=pl.BlockSpec((tm, d), row),
        out_shape=jax.ShapeDtypeStruct((t, d), F32),
        scratch_shapes=[pltpu.VMEM((tm, d), BF16)],
        compiler_params=_params(("arbitrary",)),
        name="postmix",
    )(x2d, oa2d, ob2d, *([z2d] * (2 * n_gate)), wa, wb, wo, g, gate)


def _ffn_kernel(x_ref, g_ref, sc_ref, sh_ref, w1_ref, w3_ref, w2_ref, gpost_ref, gate_ref, o_ref,
                u_scr):
    k = pl.program_id(1)
    last = pl.num_programs(1) - 1

    @pl.when(k == 0)
    def _():
        u_scr[...] = _mod_norm(x_ref[...], g_ref[...], sc_ref[...], sh_ref[...]).astype(BF16)
        o_ref[...] = jnp.zeros_like(o_ref)

    u = u_scr[...]
    h1 = jnp.dot(u, w1_ref[...], preferred_element_type=F32)
    h3 = jnp.dot(u, w3_ref[...], preferred_element_type=F32)
    h = (h1 * _sigmoid(h1) * h3).astype(BF16)
    o_ref[...] += jnp.dot(h, w2_ref[...], preferred_element_type=F32)

    @pl.when(k == last)
    def _():
        o_ref[...] = x_ref[...] + gate_ref[...] * _rms(o_ref[...], gpost_ref[...])


def _ffn(x2d, g_pre, mod_scale, mod_shift, w13, w2, g_post, gate, seq):
    t, d = x2d.shape
    tm, tf = FFN_TM, FFN_TF
    d_ff = w2.shape[0]
    nk = d_ff // tf
    nsb = seq // tm
    row = lambda i, k: (i, 0)
    const2 = lambda i, k: (0, 0)
    per_batch = lambda i, k: (i // nsb, 0, 0)
    return pl.pallas_call(
        _ffn_kernel,
        grid=(t // tm, nk),
        in_specs=[
            pl.BlockSpec((tm, d), row),
            pl.BlockSpec((1, d), const2),
            pl.BlockSpec((None, 1, d), per_batch),
            pl.BlockSpec((None, 1, d), per_batch),
            pl.BlockSpec((d, tf), lambda i, k: (0, k)),
            pl.BlockSpec((d, tf), lambda i, k: (0, nk + k)),
            pl.BlockSpec((tf, d), lambda i, k: (k, 0)),
            pl.BlockSpec((1, d), const2),
            pl.BlockSpec((None, 1, d), per_batch),
        ],
        out_specs=pl.BlockSpec((tm, d), row),
        out_shape=jax.ShapeDtypeStruct((t, d), F32),
        scratch_shapes=[pltpu.VMEM((tm, d), BF16)],
        compiler_params=_params(("arbitrary", "arbitrary")),
        name="ffn",
    )(x2d, g_pre, mod_scale, mod_shift, w13, w13, w2, g_post, gate)


def _rope_tables(seq):
    def angles(pos, dim):
        inv_freq = ROPE_THETA ** (-jnp.arange(0, dim, 2, dtype=F32) / dim)
        ang = pos.astype(F32)[:, None] * inv_freq[None, :]
        return jnp.cos(ang), jnp.sin(ang)

    t = jnp.arange(seq)
    c1, s1 = angles(t, HEAD_DIM)
    cr, sr = angles(t // GRID_W, HEAD_DIM // 2)
    cc, sc = angles(t % GRID_W, HEAD_DIM // 2)
    zero = jnp.zeros_like(sr)
    cos1 = jnp.concatenate([c1, c1], axis=-1)
    sin1 = jnp.concatenate([-s1, s1], axis=-1)
    cosx = jnp.concatenate([cr, cr, cc, cc], axis=-1)
    sin_lo = jnp.concatenate([-sr, zero, -sc, zero], axis=-1)
    sin_hi = jnp.concatenate([zero, sr, zero, sc], axis=-1)
    return cos1, sin1, cosx, sin_lo, sin_hi


def _encoder(x, mods, weights):
    b, seq, d = x.shape
    tables = _rope_tables(seq)
    x2d = x.reshape(b * seq, d)
    for l, wl in enumerate(weights):
        shift_m, scale_m, gate_m, shift_f, scale_f, gate_f = [
            mods[l][:, None, i * d:(i + 1) * d] for i in range(N_MOD)]
        z2d = _inproj(x2d, scale_m, shift_m, wl["g_pre_mix"], wl["w_in"], tables, wl["q_norm_b"],
                      wl["k_norm_b"], seq)
        z3 = z2d.reshape(b, seq, IN_W)
        oa = _window_attention(z3, wl["sink_rows"])
        ob = _dense_attention(z3)
        x2d = _postmix(x2d, oa.reshape(b * seq, -1), ob.reshape(b * seq, -1), z2d, wl["w_branch_a"],
                       wl["w_branch_b"], wl["w_out"], wl["g_post_mix"], gate_m, seq)
        x2d = _ffn(x2d, wl["g_pre_ffn"], scale_f, shift_f, wl["w_13"], wl["w_2"], wl["g_post_ffn"],
                   gate_f, seq)
    return x2d.reshape(b, seq, d)


def kernel(x_prompt, x_sample, c_prompt, c_sample, g_pre_mix, g_post_mix, g_pre_ffn, g_post_ffn, w_mod,
           b_mod, w_in, q_norm_b, k_norm_b, sink_a, w_branch_a, w_branch_b, w_out, w_13, w_2):
    depth = w_in.shape[0]
    nb_p, nb_s = c_prompt.shape[0], c_sample.shape[0]
    assert nb_p + nb_s <= MOD_ROWS
    c_all = jnp.zeros((MOD_ROWS, D_MODEL), F32)
    c_all = c_all.at[:nb_p].set(c_prompt).at[nb_p:nb_p + nb_s].set(c_sample)
    mod = _modulation(c_all, w_mod, b_mod)
    mods_p = [mod[l, :nb_p] for l in range(depth)]
    mods_s = [mod[l, nb_p:nb_p + nb_s] for l in range(depth)]

    weights = []
    for l in range(depth):
        sink_rows = jnp.broadcast_to(
            sink_a[l].astype(F32).reshape(A_KV_HEADS, 1, GROUP, 1),
            (A_KV_HEADS, 1, GROUP, WINDOW)).reshape(A_KV_HEADS, 1, GROUP * WINDOW)
        weights.append({
            "g_pre_mix": g_pre_mix[l][None], "g_post_mix": g_post_mix[l][None],
            "g_pre_ffn": g_pre_ffn[l][None], "g_post_ffn": g_post_ffn[l][None],
            "q_norm_b": q_norm_b[l][None], "k_norm_b": k_norm_b[l][None],
            "sink_rows": sink_rows,
            "w_in": jnp.concatenate([w_in[l][:, off:off + CHUNK_W] for off, _ in Z_CHUNKS],
                                    axis=1).astype(BF16),
            "w_branch_a": w_branch_a[l].astype(BF16), "w_branch_b": w_branch_b[l].astype(BF16),
            "w_out": w_out[l].astype(BF16),
            "w_13": w_13[l].astype(BF16), "w_2": w_2[l].astype(BF16),
        })

    y_prompt = _encoder(x_prompt, mods_p, weights)
    y_sample = _encoder(x_sample, mods_s, weights)
    return (y_prompt, y_sample)
```

```python
import functools

import jax
import jax.numpy as jnp
from jax import lax
from jax.experimental import pallas as pl
from jax.experimental.pallas import tpu as pltpu

D_MODEL = 2048
HEAD_DIM = 128
A_Q_HEADS = 8
A_KV_HEADS = 2
B_Q_HEADS = 8
B_KV_HEADS = 2
GROUP = A_Q_HEADS // A_KV_HEADS
WINDOW = 128
GRID_W = 64
ROPE_THETA = 10000.0
N_MOD = 6
EPS = 1e-6
MASK_VALUE = -1e30
LOG2_E = 1.4426950408889634

QA_OFF = 0
KA_OFF = QA_OFF + A_Q_HEADS * HEAD_DIM
VA_OFF = KA_OFF + A_KV_HEADS * HEAD_DIM
QB_OFF = VA_OFF + A_KV_HEADS * HEAD_DIM
KB_OFF = QB_OFF + B_Q_HEADS * HEAD_DIM
VB_OFF = KB_OFF + B_KV_HEADS * HEAD_DIM
GA_OFF = VB_OFF + B_KV_HEADS * HEAD_DIM
GB_OFF = GA_OFF + D_MODEL
IN_W = GB_OFF + D_MODEL

CHUNK_W = GROUP * HEAD_DIM
Z_CHUNKS = (
    (QA_OFF, "qa"), (GA_OFF, "gate"),
    (QA_OFF + CHUNK_W, "qa"), (GA_OFF + CHUNK_W, "gate"),
    (KA_OFF, "kva"), (GA_OFF + 2 * CHUNK_W, "gate"),
    (QB_OFF, "qb"), (GA_OFF + 3 * CHUNK_W, "gate"),
    (QB_OFF + CHUNK_W, "qb"), (GB_OFF, "gate"),
    (KB_OFF, "kvb"), (GB_OFF + CHUNK_W, "gate"),
    (GB_OFF + 2 * CHUNK_W, "gate"), (GB_OFF + 3 * CHUNK_W, "gate"),
)
assert len(Z_CHUNKS) * CHUNK_W == IN_W


def _z_chunk(src_off):
    return [off for off, _ in Z_CHUNKS].index(src_off)

V7X_VMEM_BYTES = 64 * 1024 * 1024
VMEM_LIMIT = V7X_VMEM_BYTES - 8 * 1024 * 1024

F32 = jnp.float32
BF16 = jnp.bfloat16

MOD_ROWS = 16
MOD_TN = 1024
INPROJ_TM = 1024
INPROJ_TN = 1024
INPROJ_ROWS = 512
WIN_TQ = 512
DENSE_TQ = 256
DENSE_TK = 512
DENSE_UNROLL_SHORT = 2
DENSE_UNROLL_LONG = 4
DENSE_LONG_CHUNKS = 16
POST_TM = 256
FFN_TM = 512
FFN_TF = 512


def _params(sem):
    return pltpu.CompilerParams(dimension_semantics=sem, vmem_limit_bytes=VMEM_LIMIT)


def _mod_kernel(c_ref, w_ref, b_ref, o_ref):
    c = c_ref[...]
    a = (c * jax.nn.sigmoid(c)).astype(BF16)
    w = w_ref[...].astype(BF16)
    o_ref[...] = jnp.dot(a, w, preferred_element_type=F32) + b_ref[...]


def _modulation(c_all, w_mod, b_mod):
    depth, d, n = w_mod.shape
    return pl.pallas_call(
        _mod_kernel,
        grid=(depth, n // MOD_TN),
        in_specs=[
            pl.BlockSpec((MOD_ROWS, d), lambda l, j: (0, 0)),
            pl.BlockSpec((None, d, MOD_TN), lambda l, j: (l, 0, j)),
            pl.BlockSpec((None, 1, MOD_TN), lambda l, j: (l, 0, j)),
        ],
        out_specs=pl.BlockSpec((None, MOD_ROWS, MOD_TN), lambda l, j: (l, 0, j)),
        out_shape=jax.ShapeDtypeStruct((depth, MOD_ROWS, n), F32),
        compiler_params=_params(("arbitrary", "arbitrary")),
        name="modulation",
    )(c_all, w_mod, b_mod.reshape(depth, 1, n))


def _mod_norm(x, g, scale, shift):
    ms = jnp.mean(x * x, axis=-1, keepdims=True)
    y = x * lax.rsqrt(ms + EPS) * g
    return y * (1.0 + scale) + shift


def _sigmoid(x):
    return 0.5 * jnp.tanh(0.5 * x) + 0.5


def _rms(y, g):
    ms = jnp.mean(y * y, axis=-1, keepdims=True)
    return y * lax.rsqrt(ms + EPS) * g


def _rope1d(h, cos, sin_signed):
    return h * cos + pltpu.roll(h, HEAD_DIM // 2, 1) * sin_signed


def _rope_axial(h, cos, sin_lo, sin_hi):
    q = HEAD_DIM // 4
    return h * cos + pltpu.roll(h, HEAD_DIM - q, 1) * sin_lo + pltpu.roll(h, q, 1) * sin_hi


def _inproj_kernel(x_ref, g_ref, sc_ref, sh_ref, w_ref, cos1_ref, sin1_ref, cosx_ref, sinlo_ref,
                   sinhi_ref, qn_ref, kn_ref, z_ref, u_scr):
    j = pl.program_id(1)
    qscale = HEAD_DIM ** -0.5 * LOG2_E

    def rope_a(v, rows, scale):
        r = _rope1d(v, cos1_ref[rows, :], sin1_ref[rows, :])
        return r * scale if scale != 1.0 else r

    def rope_b(v, rows, gain):
        ms = jnp.mean(v * v, axis=-1, keepdims=True)
        vn = v * lax.rsqrt(ms + EPS) * gain
        return _rope_axial(vn, cosx_ref[rows, :], sinlo_ref[rows, :], sinhi_ref[rows, :])

    q_a = lambda v, rows: rope_a(v, rows, qscale)
    k_a = lambda v, rows: rope_a(v, rows, 1.0)
    q_b = lambda v, rows: rope_b(v, rows, qn_ref[...] * qscale)
    k_b = lambda v, rows: rope_b(v, rows, kn_ref[...])
    keep = lambda v, rows: v
    head_ops = {
        "qa": [q_a] * GROUP,
        "kva": [k_a] * A_KV_HEADS + [keep] * A_KV_HEADS,
        "qb": [q_b] * GROUP,
        "kvb": [k_b] * B_KV_HEADS + [keep] * B_KV_HEADS,
    }
    chunks_per_block = INPROJ_TN // CHUNK_W
    tm = u_scr.shape[0]

    def run(kinds):
        for ci, kind in enumerate(kinds):
            cols = slice(ci * CHUNK_W, (ci + 1) * CHUNK_W)
            for r0 in range(0, tm, INPROJ_ROWS):
                rows = slice(r0, r0 + INPROJ_ROWS)
                acc = jnp.dot(u_scr[rows, :], w_ref[:, cols], preferred_element_type=F32)
                if kind == "gate":
                    z_ref[rows, cols] = _sigmoid(acc).astype(BF16)
                    continue
                for hh, op in enumerate(head_ops[kind]):
                    lo = ci * CHUNK_W + hh * HEAD_DIM
                    v = op(acc[:, hh * HEAD_DIM:(hh + 1) * HEAD_DIM], rows)
                    z_ref[rows, lo:lo + HEAD_DIM] = v.astype(BF16)

    n_blocks = len(Z_CHUNKS) // chunks_per_block
    block_kinds = [tuple(k for _, k in Z_CHUNKS[bj * chunks_per_block:(bj + 1) * chunks_per_block])
                   for bj in range(n_blocks)]
    @pl.when(j == 0)
    def _():
        u_scr[...] = _mod_norm(x_ref[...], g_ref[...], sc_ref[...], sh_ref[...]).astype(BF16)
        run(block_kinds[0])

    for kinds in dict.fromkeys(block_kinds[1:]):
        cond = functools.reduce(jnp.logical_or, [j == bj for bj in range(1, n_blocks)
                                                 if block_kinds[bj] == kinds])
        pl.when(cond)(functools.partial(run, kinds))


def _inproj(x2d, mod_scale, mod_shift, g, w_in, tables, qn, kn, seq):
    t, d = x2d.shape
    tm, tn = INPROJ_TM, INPROJ_TN
    nsb = seq // tm
    row = lambda i, j: (i, 0)
    per_batch = lambda i, j: (i // nsb, 0, 0)
    const2 = lambda i, j: (0, 0)
    pos = lambda i, j: (i % nsb, 0)
    tab_spec = pl.BlockSpec((tm, HEAD_DIM), pos)
    return pl.pallas_call(
        _inproj_kernel,
        grid=(t // tm, IN_W // tn),
        in_specs=[
            pl.BlockSpec((tm, d), row),
            pl.BlockSpec((1, d), const2),
            pl.BlockSpec((None, 1, d), per_batch),
            pl.BlockSpec((None, 1, d), per_batch),
            pl.BlockSpec((d, tn), lambda i, j: (0, j)),
            tab_spec, tab_spec, tab_spec, tab_spec, tab_spec,
            pl.BlockSpec((1, HEAD_DIM), const2),
            pl.BlockSpec((1, HEAD_DIM), const2),
        ],
        out_specs=pl.BlockSpec((tm, tn), lambda i, j: (i, j)),
        out_shape=jax.ShapeDtypeStruct((t, IN_W), BF16),
        scratch_shapes=[pltpu.VMEM((tm, d), BF16)],
        compiler_params=_params(("arbitrary", "arbitrary")),
        name="inproj",
    )(x2d, g, mod_scale, mod_shift, w_in, *tables, qn, kn)


def _window_kernel(q_ref, kp_ref, km_ref, kn_ref, vp_ref, vm_ref, vn_ref, sink_ref, o_ref, *, seq):
    qi = pl.program_id(2)
    tq = q_ref.shape[0]
    kcat = jnp.concatenate([kp_ref[...], km_ref[...], kn_ref[...]], axis=0)
    vcat = jnp.concatenate([vp_ref[...], vm_ref[...], vn_ref[...]], axis=0)
    vt = vcat.astype(F32).T.astype(BF16)
    sink = sink_ref[...] * LOG2_E
    lanes = GROUP * WINDOW
    r = lax.broadcasted_iota(jnp.int32, (1, lanes), 1) & (WINDOW - 1)
    kk = lax.broadcasted_iota(jnp.int32, (3 * WINDOW, lanes), 0)
    for sb in range(tq // WINDOW):
        base = qi * tq + (sb - 1) * WINDOW
        lo = jnp.maximum(r, -base)
        hi = jnp.minimum(r + 2 * WINDOW, seq - 1 - base)
        span = lax.bitcast_convert_type(hi - lo, jnp.uint32)
        valid = lax.bitcast_convert_type(kk - lo, jnp.uint32) <= span
        qb = q_ref[sb * WINDOW:(sb + 1) * WINDOW, :]
        qs = jnp.concatenate([qb[:, g * HEAD_DIM:(g + 1) * HEAD_DIM] for g in range(GROUP)], axis=0)
        kj = kcat[sb * WINDOW:(sb + 3) * WINDOW]
        s = lax.dot_general(kj, qs, (((1,), (1,)), ((), ())), preferred_element_type=F32)
        s = jnp.where(valid, s, MASK_VALUE)
        m = jnp.maximum(jnp.max(s, axis=0, keepdims=True), sink)
        p = jnp.exp2(s - m)
        denom = jnp.sum(p, axis=0, keepdims=True) + jnp.exp2(sink - m)
        ot = jnp.dot(vt[:, sb * WINDOW:(sb + 3) * WINDOW], p.astype(BF16),
                     preferred_element_type=F32) / denom
        for g in range(GROUP):
            o_ref[sb * WINDOW:(sb + 1) * WINDOW, g * HEAD_DIM:(g + 1) * HEAD_DIM] = (
                ot[:, g * WINDOW:(g + 1) * WINDOW].T.astype(BF16))


def _window_attention(z3, sink_rows):
    b, seq, _ = z3.shape
    tq = WIN_TQ
    r = tq // WINDOW
    nblk = seq // WINDOW
    gw = CHUNK_W
    heads_per_chunk = CHUNK_W // HEAD_DIM
    k0 = _z_chunk(KA_OFF) * heads_per_chunk
    v0 = k0 + A_KV_HEADS
    q_chunks = [_z_chunk(QA_OFF + g * CHUNK_W) for g in range(A_KV_HEADS)]
    assert q_chunks[1] - q_chunks[0] == 2
    prev = lambda off: (lambda bi, h, qi: (bi, jnp.maximum(qi * r - 1, 0), off + h))
    main = lambda off: (lambda bi, h, qi: (bi, qi, off + h))
    nxt = lambda off: (lambda bi, h, qi: (bi, jnp.minimum((qi + 1) * r, nblk - 1), off + h))
    edge = lambda f: pl.BlockSpec((None, WINDOW, HEAD_DIM), f)
    body = lambda f: pl.BlockSpec((None, tq, HEAD_DIM), f)
    return pl.pallas_call(
        functools.partial(_window_kernel, seq=seq),
        grid=(b, A_KV_HEADS, seq // tq),
        in_specs=[
            pl.BlockSpec((None, tq, gw), lambda bi, h, qi: (bi, qi, q_chunks[0] + 2 * h)),
            edge(prev(k0)), body(main(k0)), edge(nxt(k0)),
            edge(prev(v0)), body(main(v0)), edge(nxt(v0)),
            pl.BlockSpec((None, 1, GROUP * WINDOW), lambda bi, h, qi: (h, 0, 0)),
        ],
        out_specs=pl.BlockSpec((None, tq, gw), lambda bi, h, qi: (bi, qi, h)),
        out_shape=jax.ShapeDtypeStruct((b, seq, A_Q_HEADS * HEAD_DIM), BF16),
        compiler_params=_params(("arbitrary", "arbitrary", "arbitrary")),
        name="window_attn",
    )(z3, z3, z3, z3, z3, z3, z3, sink_rows)


def _dense_kernel(q_ref, k_ref, v_ref, o_ref, vt_scr, s0_scr, s1_scr):
    tq = q_ref.shape[0]
    seq = k_ref.shape[0]
    cols = GROUP * tq

    @pl.when(pl.program_id(2) == 0)
    def _():
        for c in range(seq // DENSE_TK):
            blk = v_ref[c * DENSE_TK:(c + 1) * DENSE_TK, :].astype(F32)
            vt_scr[:, c * DENSE_TK:(c + 1) * DENSE_TK] = blk.T.astype(BF16)

    q = jnp.concatenate([q_ref[:, g * HEAD_DIM:(g + 1) * HEAD_DIM] for g in range(GROUP)], axis=0)

    def scores(ci, s_scr):
        start = pl.multiple_of(ci * DENSE_TK, DENSE_TK)
        k = k_ref[pl.ds(start, DENSE_TK), :]
        s = lax.dot_general(k, q, (((1,), (1,)), ((), ())), preferred_element_type=F32)
        s_scr[...] = s
        return jnp.max(s, axis=0, keepdims=True)

    def update(ci, s_scr, mc, state):
        m, l, acc = state
        start = pl.multiple_of(ci * DENSE_TK, DENSE_TK)
        vt = vt_scr[:, pl.ds(start, DENSE_TK)]
        m_new = jnp.maximum(m, mc)
        alpha = jnp.exp2(m - m_new)
        p = jnp.exp2(s_scr[...] - m_new)
        l = alpha * l + jnp.sum(p, axis=0, keepdims=True)
        acc = alpha * acc + jnp.dot(vt, p.astype(BF16), preferred_element_type=F32)
        return m_new, l, acc

    n_chunks = seq // DENSE_TK
    bufs = (s0_scr, s1_scr)
    unroll = DENSE_UNROLL_LONG if n_chunks >= DENSE_LONG_CHUNKS else DENSE_UNROLL_SHORT
    unroll = min(unroll, n_chunks)
    assert n_chunks % unroll == 0 and (unroll % 2 == 0 or n_chunks == 1)

    def group(base, mc, state, last):
        for j in range(unroll):
            c = base + j
            nxt = None if (last and j == unroll - 1) else scores(c + 1, bufs[(j + 1) % 2])
            state = update(c, bufs[j % 2], mc, state)
            mc = nxt
        return mc, state

    state = (jnp.full((1, cols), -jnp.inf, F32), jnp.zeros((1, cols), F32),
             jnp.zeros((HEAD_DIM, cols), F32))
    mc, state = lax.fori_loop(
        0, n_chunks // unroll - 1,
        lambda i, carry: group(i * unroll, carry[0], carry[1], False),
        (scores(0, bufs[0]), state))
    _, (_, l, acc) = group(n_chunks - unroll, mc, state, True)
    o = acc / l
    for g in range(GROUP):
        o_ref[:, g * HEAD_DIM:(g + 1) * HEAD_DIM] = o[:, g * tq:(g + 1) * tq].T.astype(BF16)


def _dense_attention(z3):
    b, seq, _ = z3.shape
    tq = DENSE_TQ
    gw = CHUNK_W
    heads_per_chunk = CHUNK_W // HEAD_DIM
    k0 = _z_chunk(KB_OFF) * heads_per_chunk
    v0 = k0 + B_KV_HEADS
    q_chunks = [_z_chunk(QB_OFF + g * CHUNK_W) for g in range(B_KV_HEADS)]
    assert q_chunks[1] - q_chunks[0] == 2
    return pl.pallas_call(
        _dense_kernel,
        grid=(b, B_KV_HEADS, seq // tq),
        in_specs=[
            pl.BlockSpec((None, tq, gw), lambda bi, h, qi: (bi, qi, q_chunks[0] + 2 * h)),
            pl.BlockSpec((None, seq, HEAD_DIM), lambda bi, h, qi: (bi, 0, k0 + h)),
            pl.BlockSpec((None, seq, HEAD_DIM), lambda bi, h, qi: (bi, 0, v0 + h)),
        ],
        out_specs=pl.BlockSpec((None, tq, gw), lambda bi, h, qi: (bi, qi, h)),
        out_shape=jax.ShapeDtypeStruct((b, seq, B_Q_HEADS * HEAD_DIM), BF16),
        scratch_shapes=[pltpu.VMEM((HEAD_DIM, seq), BF16),
                        pltpu.VMEM((DENSE_TK, GROUP * tq), F32),
                        pltpu.VMEM((DENSE_TK, GROUP * tq), F32)],
        compiler_params=_params(("arbitrary", "arbitrary", "arbitrary")),
        name="dense_attn",
    )(z3, z3, z3)


def _postmix_kernel(x_ref, oa_ref, ob_ref, *rest):
    n_gate = D_MODEL // CHUNK_W
    ga_refs, gb_refs = rest[:n_gate], rest[n_gate:2 * n_gate]
    wa_ref, wb_ref, wo_ref, g_ref, gate_ref, o_ref, m_scr = rest[2 * n_gate:]
    oa = oa_ref[...]
    ob = ob_ref[...]
    for ci, (ga_ref, gb_ref) in enumerate(zip(ga_refs, gb_refs)):
        cs = slice(ci * CHUNK_W, (ci + 1) * CHUNK_W)
        pa = jnp.dot(oa, wa_ref[:, cs], preferred_element_type=F32)
        pb = jnp.dot(ob, wb_ref[:, cs], preferred_element_type=F32)
        merged = ga_ref[...].astype(F32) * pa + gb_ref[...].astype(F32) * pb
        m_scr[:, cs] = merged.astype(BF16)
    y = jnp.dot(m_scr[...], wo_ref[...], preferred_element_type=F32)
    o_ref[...] = x_ref[...] + gate_ref[...] * _rms(y, g_ref[...])


def _postmix(x2d, oa2d, ob2d, z2d, wa, wb, wo, g, gate, seq):
    t, d = x2d.shape
    tm = POST_TM
    nsb = seq // tm
    n_gate = d // CHUNK_W
    row = lambda i: (i, 0)
    const2 = lambda i: (0, 0)
    resident = lambda shape: pl.BlockSpec(shape, const2, pipeline_mode=pl.Buffered(1))
    gate_blk = lambda off: pl.BlockSpec((tm, CHUNK_W), functools.partial(
        lambda chunk, i: (i, chunk), _z_chunk(off)))
    gate_specs = ([gate_blk(GA_OFF + c * CHUNK_W) for c in range(n_gate)]
                  + [gate_blk(GB_OFF + c * CHUNK_W) for c in range(n_gate)])
    return pl.pallas_call(
        _postmix_kernel,
        grid=(t // tm,),
        in_specs=[
            pl.BlockSpec((tm, d), row),
            pl.BlockSpec((tm, oa2d.shape[1]), row),
            pl.BlockSpec((tm, ob2d.shape[1]), row),
            *gate_specs,
            resident(wa.shape), resident(wb.shape), resident(wo.shape),
            pl.BlockSpec((1, d), const2),
            pl.BlockSpec((None, 1, d), lambda i: (i // nsb, 0, 0)),
        ],
        out_specs=pl.BlockSpec((tm, d), row),
        out_shape=jax.ShapeDtypeStruct((t, d), F32),
        scratch_shapes=[pltpu.VMEM((tm, d), BF16)],
        compiler_params=_params(("arbitrary",)),
        name="postmix",
    )(x2d, oa2d, ob2d, *([z2d] * (2 * n_gate)), wa, wb, wo, g, gate)


def _ffn_kernel(x_ref, g_ref, sc_ref, sh_ref, w1_ref, w3_ref, w2_ref, gpost_ref, gate_ref, o_ref,
                u_scr):
    k = pl.program_id(1)
    last = pl.num_programs(1) - 1

    def chunk(first):
        u = u_scr[...]
        h1 = jnp.dot(u, w1_ref[...], preferred_element_type=F32)
        h3 = jnp.dot(u, w3_ref[...], preferred_element_type=F32)
        h = (h1 * _sigmoid(h1) * h3).astype(BF16)
        part = jnp.dot(h, w2_ref[...], preferred_element_type=F32)
        if first:
            o_ref[...] = part
        else:
            o_ref[...] += part

    @pl.when(k == 0)
    def _():
        u_scr[...] = _mod_norm(x_ref[...], g_ref[...], sc_ref[...], sh_ref[...]).astype(BF16)
        chunk(True)

    pl.when(k > 0)(functools.partial(chunk, False))

    @pl.when(k == last)
    def _():
        o_ref[...] = x_ref[...] + gate_ref[...] * _rms(o_ref[...], gpost_ref[...])


def _ffn(x2d, g_pre, mod_scale, mod_shift, w13, w2, g_post, gate, seq):
    t, d = x2d.shape
    tm, tf = FFN_TM, FFN_TF
    d_ff = w2.shape[0]
    nk = d_ff // tf
    nsb = seq // tm
    row = lambda i, k: (i, 0)
    const2 = lambda i, k: (0, 0)
    per_batch = lambda i, k: (i // nsb, 0, 0)
    return pl.pallas_call(
        _ffn_kernel,
        grid=(t // tm, nk),
        in_specs=[
            pl.BlockSpec((tm, d), row),
            pl.BlockSpec((1, d), const2),
            pl.BlockSpec((None, 1, d), per_batch),
            pl.BlockSpec((None, 1, d), per_batch),
            pl.BlockSpec((d, tf), lambda i, k: (0, k)),
            pl.BlockSpec((d, tf), lambda i, k: (0, nk + k)),
            pl.BlockSpec((tf, d), lambda i, k: (k, 0)),
            pl.BlockSpec((1, d), const2),
            pl.BlockSpec((None, 1, d), per_batch),
        ],
        out_specs=pl.BlockSpec((tm, d), row),
        out_shape=jax.ShapeDtypeStruct((t, d), F32),
        scratch_shapes=[pltpu.VMEM((tm, d), BF16)],
        compiler_params=_params(("arbitrary", "arbitrary")),
        name="ffn",
    )(x2d, g_pre, mod_scale, mod_shift, w13, w13, w2, g_post, gate)


def _rope_tables(seq):
    def angles(pos, dim):
        inv_freq = ROPE_THETA ** (-jnp.arange(0, dim, 2, dtype=F32) / dim)
        ang = pos.astype(F32)[:, None] * inv_freq[None, :]
        return jnp.cos(ang), jnp.sin(ang)

    t = jnp.arange(seq)
    c1, s1 = angles(t, HEAD_DIM)
    cr, sr = angles(t // GRID_W, HEAD_DIM // 2)
    cc, sc = angles(t % GRID_W, HEAD_DIM // 2)
    zero = jnp.zeros_like(sr)
    cos1 = jnp.concatenate([c1, c1], axis=-1)
    sin1 = jnp.concatenate([-s1, s1], axis=-1)
    cosx = jnp.concatenate([cr, cr, cc, cc], axis=-1)
    sin_lo = jnp.concatenate([-sr, zero, -sc, zero], axis=-1)
    sin_hi = jnp.concatenate([zero, sr, zero, sc], axis=-1)
    return cos1, sin1, cosx, sin_lo, sin_hi


def _encoder(x, mods, weights):
    b, seq, d = x.shape
    tables = _rope_tables(seq)
    x2d = x.reshape(b * seq, d)
    for l, wl in enumerate(weights):
        shift_m, scale_m, gate_m, shift_f, scale_f, gate_f = [
            mods[l][:, None, i * d:(i + 1) * d] for i in range(N_MOD)]
        z2d = _inproj(x2d, scale_m, shift_m, wl["g_pre_mix"], wl["w_in"], tables, wl["q_norm_b"],
                      wl["k_norm_b"], seq)
        z3 = z2d.reshape(b, seq, IN_W)
        oa = _window_attention(z3, wl["sink_rows"])
        ob = _dense_attention(z3)
        x2d = _postmix(x2d, oa.reshape(b * seq, -1), ob.reshape(b * seq, -1), z2d, wl["w_branch_a"],
                       wl["w_branch_b"], wl["w_out"], wl["g_post_mix"], gate_m, seq)
        x2d = _ffn(x2d, wl["g_pre_ffn"], scale_f, shift_f, wl["w_13"], wl["w_2"], wl["g_post_ffn"],
                   gate_f, seq)
    return x2d.reshape(b, seq, d)


def kernel(x_prompt, x_sample, c_prompt, c_sample, g_pre_mix, g_post_mix, g_pre_ffn, g_post_ffn, w_mod,
           b_mod, w_in, q_norm_b, k_norm_b, sink_a, w_branch_a, w_branch_b, w_out, w_13, w_2):
    depth = w_in.shape[0]
    nb_p, nb_s = c_prompt.shape[0], c_sample.shape[0]
    assert nb_p + nb_s <= MOD_ROWS
    c_all = jnp.zeros((MOD_ROWS, D_MODEL), F32)
    c_all = c_all.at[:nb_p].set(c_prompt).at[nb_p:nb_p + nb_s].set(c_sample)
    mod = _modulation(c_all, w_mod, b_mod)
    mods_p = [mod[l, :nb_p] for l in range(depth)]
    mods_s = [mod[l, nb_p:nb_p + nb_s] for l in range(depth)]

    weights = []
    for l in range(depth):
        sink_rows = jnp.broadcast_to(
            sink_a[l].astype(F32).reshape(A_KV_HEADS, 1, GROUP, 1),
            (A_KV_HEADS, 1, GROUP, WINDOW)).reshape(A_KV_HEADS, 1, GROUP * WINDOW)
        weights.append({
            "g_pre_mix": g_pre_mix[l][None], "g_post_mix": g_post_mix[l][None],
            "g_pre_ffn": g_pre_ffn[l][None], "g_post_ffn": g_post_ffn[l][None],
            "q_norm_b": q_norm_b[l][None], "k_norm_b": k_norm_b[l][None],
            "sink_rows": sink_rows,
            "w_in": jnp.concatenate([w_in[l][:, off:off + CHUNK_W] for off, _ in Z_CHUNKS],
                                    axis=1).astype(BF16),
            "w_branch_a": w_branch_a[l].astype(BF16), "w_branch_b": w_branch_b[l].astype(BF16),
            "w_out": w_out[l].astype(BF16),
            "w_13": w_13[l].astype(BF16), "w_2": w_2[l].astype(BF16),
        })

    y_prompt = _encoder(x_prompt, mods_p, weights)
    y_sample = _encoder(x_sample, mods_s, weights)
    return (y_prompt, y_sample)
```

```python
import functools

import jax
import jax.numpy as jnp
from jax import lax
from jax.experimental import pallas as pl
from jax.experimental.pallas import tpu as pltpu

D_MODEL = 2048
HEAD_DIM = 128
A_Q_HEADS = 8
A_KV_HEADS = 2
B_Q_HEADS = 8
B_KV_HEADS = 2
GROUP = A_Q_HEADS // A_KV_HEADS
WINDOW = 128
GRID_W = 64
ROPE_THETA = 10000.0
N_MOD = 6
EPS = 1e-6
MASK_VALUE = -1e30
LOG2_E = 1.4426950408889634

QA_OFF = 0
KA_OFF = QA_OFF + A_Q_HEADS * HEAD_DIM
VA_OFF = KA_OFF + A_KV_HEADS * HEAD_DIM
QB_OFF = VA_OFF + A_KV_HEADS * HEAD_DIM
KB_OFF = QB_OFF + B_Q_HEADS * HEAD_DIM
VB_OFF = KB_OFF + B_KV_HEADS * HEAD_DIM
GA_OFF = VB_OFF + B_KV_HEADS * HEAD_DIM
GB_OFF = GA_OFF + D_MODEL
IN_W = GB_OFF + D_MODEL

CHUNK_W = GROUP * HEAD_DIM
Z_CHUNKS = (
    (QA_OFF, "qa"), (GA_OFF, "gate"),
    (QA_OFF + CHUNK_W, "qa"), (GA_OFF + CHUNK_W, "gate"),
    (KA_OFF, "kva"), (GA_OFF + 2 * CHUNK_W, "gate"),
    (QB_OFF, "qb"), (GA_OFF + 3 * CHUNK_W, "gate"),
    (QB_OFF + CHUNK_W, "qb"), (GB_OFF, "gate"),
    (KB_OFF, "kvb"), (GB_OFF + CHUNK_W, "gate"),
    (GB_OFF + 2 * CHUNK_W, "gate"), (GB_OFF + 3 * CHUNK_W, "gate"),
)
assert len(Z_CHUNKS) * CHUNK_W == IN_W


def _z_chunk(src_off):
    return [off for off, _ in Z_CHUNKS].index(src_off)

V7X_VMEM_BYTES = 64 * 1024 * 1024
VMEM_LIMIT = V7X_VMEM_BYTES - 8 * 1024 * 1024

F32 = jnp.float32
BF16 = jnp.bfloat16

MOD_ROWS = 16
MOD_TN = 1024
INPROJ_TM = 1024
INPROJ_TN = 1024
INPROJ_ROWS = 512
WIN_TQ = 512
DENSE_TQ = 256
DENSE_TK = 512
DENSE_ONES_ROWS = 16
DENSE_UNROLL_SHORT = 2
DENSE_UNROLL_LONG = 4
DENSE_LONG_CHUNKS = 16
POST_TM = 512
POST_ROWS = 256
FFN_TM = 1024
FFN_TF = 512


def _params(sem):
    return pltpu.CompilerParams(dimension_semantics=sem, vmem_limit_bytes=VMEM_LIMIT)


def _mod_kernel(c_ref, w_ref, b_ref, o_ref):
    c = c_ref[...]
    a = (c * jax.nn.sigmoid(c)).astype(BF16)
    w = w_ref[...].astype(BF16)
    o_ref[...] = jnp.dot(a, w, preferred_element_type=F32) + b_ref[...]


def _modulation(c_all, w_mod, b_mod):
    depth, d, n = w_mod.shape
    return pl.pallas_call(
        _mod_kernel,
        grid=(depth, n // MOD_TN),
        in_specs=[
            pl.BlockSpec((MOD_ROWS, d), lambda l, j: (0, 0)),
            pl.BlockSpec((None, d, MOD_TN), lambda l, j: (l, 0, j)),
            pl.BlockSpec((None, 1, MOD_TN), lambda l, j: (l, 0, j)),
        ],
        out_specs=pl.BlockSpec((None, MOD_ROWS, MOD_TN), lambda l, j: (l, 0, j)),
        out_shape=jax.ShapeDtypeStruct((depth, MOD_ROWS, n), F32),
        compiler_params=_params(("arbitrary", "arbitrary")),
        name="modulation",
    )(c_all, w_mod, b_mod.reshape(depth, 1, n))


def _mod_norm(x, g, scale, shift):
    ms = jnp.mean(x * x, axis=-1, keepdims=True)
    y = x * lax.rsqrt(ms + EPS) * g
    return y * (1.0 + scale) + shift


def _sigmoid(x):
    return 0.5 * jnp.tanh(0.5 * x) + 0.5


def _rms(y, g):
    ms = jnp.mean(y * y, axis=-1, keepdims=True)
    return y * lax.rsqrt(ms + EPS) * g


def _rope1d(h, cos, sin_signed):
    return h * cos + pltpu.roll(h, HEAD_DIM // 2, 1) * sin_signed


def _rope_axial(h, cos, sin_lo, sin_hi):
    q = HEAD_DIM // 4
    return h * cos + pltpu.roll(h, HEAD_DIM - q, 1) * sin_lo + pltpu.roll(h, q, 1) * sin_hi


def _inproj_kernel(x_ref, g_ref, sc_ref, sh_ref, w_ref, cos1_ref, sin1_ref, cosx_ref, sinlo_ref,
                   sinhi_ref, qn_ref, kn_ref, z_ref, u_scr):
    j = pl.program_id(1)
    qscale = HEAD_DIM ** -0.5 * LOG2_E

    def rope_a(v, rows, scale):
        r = _rope1d(v, cos1_ref[rows, :], sin1_ref[rows, :])
        return r * scale if scale != 1.0 else r

    def rope_b(v, rows, gain):
        ms = jnp.mean(v * v, axis=-1, keepdims=True)
        vn = v * lax.rsqrt(ms + EPS) * gain
        return _rope_axial(vn, cosx_ref[rows, :], sinlo_ref[rows, :], sinhi_ref[rows, :])

    q_a = lambda v, rows: rope_a(v, rows, qscale)
    k_a = lambda v, rows: rope_a(v, rows, 1.0)
    q_b = lambda v, rows: rope_b(v, rows, qn_ref[...] * qscale)
    k_b = lambda v, rows: rope_b(v, rows, kn_ref[...])
    keep = lambda v, rows: v
    head_ops = {
        "qa": [q_a] * GROUP,
        "kva": [k_a] * A_KV_HEADS + [keep] * A_KV_HEADS,
        "qb": [q_b] * GROUP,
        "kvb": [k_b] * B_KV_HEADS + [keep] * B_KV_HEADS,
    }
    chunks_per_block = INPROJ_TN // CHUNK_W
    tm = u_scr.shape[0]

    def run(kinds):
        for ci, kind in enumerate(kinds):
            cols = slice(ci * CHUNK_W, (ci + 1) * CHUNK_W)
            for r0 in range(0, tm, INPROJ_ROWS):
                rows = slice(r0, r0 + INPROJ_ROWS)
                acc = jnp.dot(u_scr[rows, :], w_ref[:, cols], preferred_element_type=F32)
                if kind == "gate":
                    z_ref[rows, cols] = _sigmoid(acc).astype(BF16)
                    continue
                for hh, op in enumerate(head_ops[kind]):
                    lo = ci * CHUNK_W + hh * HEAD_DIM
                    v = op(acc[:, hh * HEAD_DIM:(hh + 1) * HEAD_DIM], rows)
                    z_ref[rows, lo:lo + HEAD_DIM] = v.astype(BF16)

    n_blocks = len(Z_CHUNKS) // chunks_per_block
    block_kinds = [tuple(k for _, k in Z_CHUNKS[bj * chunks_per_block:(bj + 1) * chunks_per_block])
                   for bj in range(n_blocks)]
    @pl.when(j == 0)
    def _():
        u_scr[...] = _mod_norm(x_ref[...], g_ref[...], sc_ref[...], sh_ref[...]).astype(BF16)
        run(block_kinds[0])

    for kinds in dict.fromkeys(block_kinds[1:]):
        cond = functools.reduce(jnp.logical_or, [j == bj for bj in range(1, n_blocks)
                                                 if block_kinds[bj] == kinds])
        pl.when(cond)(functools.partial(run, kinds))


def _inproj(x2d, mod_scale, mod_shift, g, w_in, tables, qn, kn, seq):
    t, d = x2d.shape
    tm, tn = INPROJ_TM, INPROJ_TN
    nsb = seq // tm
    row = lambda i, j: (i, 0)
    per_batch = lambda i, j: (i // nsb, 0, 0)
    const2 = lambda i, j: (0, 0)
    pos = lambda i, j: (i % nsb, 0)
    tab_spec = pl.BlockSpec((tm, HEAD_DIM), pos)
    return pl.pallas_call(
        _inproj_kernel,
        grid=(t // tm, IN_W // tn),
        in_specs=[
            pl.BlockSpec((tm, d), row),
            pl.BlockSpec((1, d), const2),
            pl.BlockSpec((None, 1, d), per_batch),
            pl.BlockSpec((None, 1, d), per_batch),
            pl.BlockSpec((d, tn), lambda i, j: (0, j)),
            tab_spec, tab_spec, tab_spec, tab_spec, tab_spec,
            pl.BlockSpec((1, HEAD_DIM), const2),
            pl.BlockSpec((1, HEAD_DIM), const2),
        ],
        out_specs=pl.BlockSpec((tm, tn), lambda i, j: (i, j)),
        out_shape=jax.ShapeDtypeStruct((t, IN_W), BF16),
        scratch_shapes=[pltpu.VMEM((tm, d), BF16)],
        compiler_params=_params(("arbitrary", "arbitrary")),
        name="inproj",
    )(x2d, g, mod_scale, mod_shift, w_in, *tables, qn, kn)


def _window_kernel(q_ref, kp_ref, km_ref, kn_ref, vp_ref, vm_ref, vn_ref, sink_ref, o_ref, *, seq):
    qi = pl.program_id(2)
    tq = q_ref.shape[0]
    kcat = jnp.concatenate([kp_ref[...], km_ref[...], kn_ref[...]], axis=0)
    vcat = jnp.concatenate([vp_ref[...], vm_ref[...], vn_ref[...]], axis=0)
    vt = vcat.astype(F32).T.astype(BF16)
    sink = sink_ref[...] * LOG2_E
    lanes = GROUP * WINDOW
    r = lax.broadcasted_iota(jnp.int32, (1, lanes), 1) & (WINDOW - 1)
    kk = lax.broadcasted_iota(jnp.int32, (3 * WINDOW, lanes), 0)
    for sb in range(tq // WINDOW):
        base = qi * tq + (sb - 1) * WINDOW
        lo = jnp.maximum(r, -base)
        hi = jnp.minimum(r + 2 * WINDOW, seq - 1 - base)
        span = lax.bitcast_convert_type(hi - lo, jnp.uint32)
        valid = lax.bitcast_convert_type(kk - lo, jnp.uint32) <= span
        qb = q_ref[sb * WINDOW:(sb + 1) * WINDOW, :]
        qs = jnp.concatenate([qb[:, g * HEAD_DIM:(g + 1) * HEAD_DIM] for g in range(GROUP)], axis=0)
        kj = kcat[sb * WINDOW:(sb + 3) * WINDOW]
        s = lax.dot_general(kj, qs, (((1,), (1,)), ((), ())), preferred_element_type=F32)
        s = jnp.where(valid, s, MASK_VALUE)
        m = jnp.maximum(jnp.max(s, axis=0, keepdims=True), sink)
        p = jnp.exp2(s - m)
        denom = jnp.sum(p, axis=0, keepdims=True) + jnp.exp2(sink - m)
        ot = jnp.dot(vt[:, sb * WINDOW:(sb + 3) * WINDOW], p.astype(BF16),
                     preferred_element_type=F32) / denom
        for g in range(GROUP):
            o_ref[sb * WINDOW:(sb + 1) * WINDOW, g * HEAD_DIM:(g + 1) * HEAD_DIM] = (
                ot[:, g * WINDOW:(g + 1) * WINDOW].T.astype(BF16))


def _window_attention(z3, sink_rows):
    b, seq, _ = z3.shape
    tq = WIN_TQ
    r = tq // WINDOW
    nblk = seq // WINDOW
    gw = CHUNK_W
    heads_per_chunk = CHUNK_W // HEAD_DIM
    k0 = _z_chunk(KA_OFF) * heads_per_chunk
    v0 = k0 + A_KV_HEADS
    q_chunks = [_z_chunk(QA_OFF + g * CHUNK_W) for g in range(A_KV_HEADS)]
    assert q_chunks[1] - q_chunks[0] == 2
    prev = lambda off: (lambda bi, h, qi: (bi, jnp.maximum(qi * r - 1, 0), off + h))
    main = lambda off: (lambda bi, h, qi: (bi, qi, off + h))
    nxt = lambda off: (lambda bi, h, qi: (bi, jnp.minimum((qi + 1) * r, nblk - 1), off + h))
    edge = lambda f: pl.BlockSpec((None, WINDOW, HEAD_DIM), f)
    body = lambda f: pl.BlockSpec((None, tq, HEAD_DIM), f)
    return pl.pallas_call(
        functools.partial(_window_kernel, seq=seq),
        grid=(b, A_KV_HEADS, seq // tq),
        in_specs=[
            pl.BlockSpec((None, tq, gw), lambda bi, h, qi: (bi, qi, q_chunks[0] + 2 * h)),
            edge(prev(k0)), body(main(k0)), edge(nxt(k0)),
            edge(prev(v0)), body(main(v0)), edge(nxt(v0)),
            pl.BlockSpec((None, 1, GROUP * WINDOW), lambda bi, h, qi: (h, 0, 0)),
        ],
        out_specs=pl.BlockSpec((None, tq, gw), lambda bi, h, qi: (bi, qi, h)),
        out_shape=jax.ShapeDtypeStruct((b, seq, A_Q_HEADS * HEAD_DIM), BF16),
        compiler_params=_params(("arbitrary", "arbitrary", "arbitrary")),
        name="window_attn",
    )(z3, z3, z3, z3, z3, z3, z3, sink_rows)


def _dense_kernel(q_ref, k_ref, v_ref, o_ref, vt_scr, s0_scr, s1_scr):
    tq = q_ref.shape[0]
    seq = k_ref.shape[0]
    cols = GROUP * tq

    @pl.when(pl.program_id(2) == 0)
    def _():
        for c in range(seq // DENSE_TK):
            blk = v_ref[c * DENSE_TK:(c + 1) * DENSE_TK, :].astype(F32)
            vt_scr[:HEAD_DIM, c * DENSE_TK:(c + 1) * DENSE_TK] = blk.T.astype(BF16)
        vt_scr[HEAD_DIM:, :] = jnp.ones((DENSE_ONES_ROWS, seq), BF16)

    q = jnp.concatenate([q_ref[:, g * HEAD_DIM:(g + 1) * HEAD_DIM] for g in range(GROUP)], axis=0)

    def scores(ci, s_scr):
        start = pl.multiple_of(ci * DENSE_TK, DENSE_TK)
        k = k_ref[pl.ds(start, DENSE_TK), :]
        s = lax.dot_general(k, q, (((1,), (1,)), ((), ())), preferred_element_type=F32)
        s_scr[...] = s
        return jnp.max(s, axis=0, keepdims=True)

    def update(ci, s_scr, mc, state):
        m, acc = state
        start = pl.multiple_of(ci * DENSE_TK, DENSE_TK)
        vt = vt_scr[:, pl.ds(start, DENSE_TK)]
        m_new = jnp.maximum(m, mc)
        alpha = jnp.exp2(m - m_new)
        p = jnp.exp2(s_scr[...] - m_new).astype(BF16)
        acc = alpha * acc + jnp.dot(vt, p, preferred_element_type=F32)
        return m_new, acc

    n_chunks = seq // DENSE_TK
    bufs = (s0_scr, s1_scr)
    unroll = DENSE_UNROLL_LONG if n_chunks >= DENSE_LONG_CHUNKS else DENSE_UNROLL_SHORT
    unroll = min(unroll, n_chunks)
    assert n_chunks % unroll == 0 and (unroll % 2 == 0 or n_chunks == 1)

    def group(base, mc, state, last):
        for j in range(unroll):
            c = base + j
            nxt = None if (last and j == unroll - 1) else scores(c + 1, bufs[(j + 1) % 2])
            state = update(c, bufs[j % 2], mc, state)
            mc = nxt
        return mc, state

    state = (jnp.full((1, cols), -jnp.inf, F32),
             jnp.zeros((HEAD_DIM + DENSE_ONES_ROWS, cols), F32))
    mc, state = lax.fori_loop(
        0, n_chunks // unroll - 1,
        lambda i, carry: group(i * unroll, carry[0], carry[1], False),
        (scores(0, bufs[0]), state))
    _, (_, acc) = group(n_chunks - unroll, mc, state, True)
    o = acc[:HEAD_DIM] / acc[HEAD_DIM:HEAD_DIM + 1]
    for g in range(GROUP):
        o_ref[:, g * HEAD_DIM:(g + 1) * HEAD_DIM] = o[:, g * tq:(g + 1) * tq].T.astype(BF16)


def _dense_attention(z3):
    b, seq, _ = z3.shape
    tq = DENSE_TQ
    gw = CHUNK_W
    heads_per_chunk = CHUNK_W // HEAD_DIM
    k0 = _z_chunk(KB_OFF) * heads_per_chunk
    v0 = k0 + B_KV_HEADS
    q_chunks = [_z_chunk(QB_OFF + g * CHUNK_W) for g in range(B_KV_HEADS)]
    assert q_chunks[1] - q_chunks[0] == 2
    return pl.pallas_call(
        _dense_kernel,
        grid=(b, B_KV_HEADS, seq // tq),
        in_specs=[
            pl.BlockSpec((None, tq, gw), lambda bi, h, qi: (bi, qi, q_chunks[0] + 2 * h)),
            pl.BlockSpec((None, seq, HEAD_DIM), lambda bi, h, qi: (bi, 0, k0 + h)),
            pl.BlockSpec((None, seq, HEAD_DIM), lambda bi, h, qi: (bi, 0, v0 + h)),
        ],
        out_specs=pl.BlockSpec((None, tq, gw), lambda bi, h, qi: (bi, qi, h)),
        out_shape=jax.ShapeDtypeStruct((b, seq, B_Q_HEADS * HEAD_DIM), BF16),
        scratch_shapes=[pltpu.VMEM((HEAD_DIM + DENSE_ONES_ROWS, seq), BF16),
                        pltpu.VMEM((DENSE_TK, GROUP * tq), F32),
                        pltpu.VMEM((DENSE_TK, GROUP * tq), F32)],
        compiler_params=_params(("arbitrary", "arbitrary", "arbitrary")),
        name="dense_attn",
    )(z3, z3, z3)


def _postmix_kernel(x_ref, oa_ref, ob_ref, *rest):
    n_gate = D_MODEL // CHUNK_W
    ga_refs, gb_refs = rest[:n_gate], rest[n_gate:2 * n_gate]
    wa_ref, wb_ref, wo_ref, g_ref, gate_ref, o_ref, m_scr = rest[2 * n_gate:]
    for r0 in range(0, x_ref.shape[0], POST_ROWS):
        rows = slice(r0, r0 + POST_ROWS)
        oa = oa_ref[rows, :]
        ob = ob_ref[rows, :]
        for ci, (ga_ref, gb_ref) in enumerate(zip(ga_refs, gb_refs)):
            cs = slice(ci * CHUNK_W, (ci + 1) * CHUNK_W)
            pa = jnp.dot(oa, wa_ref[:, cs], preferred_element_type=F32)
            pb = jnp.dot(ob, wb_ref[:, cs], preferred_element_type=F32)
            merged = ga_ref[rows, :].astype(F32) * pa + gb_ref[rows, :].astype(F32) * pb
            m_scr[rows, cs] = merged.astype(BF16)
        y = jnp.dot(m_scr[rows, :], wo_ref[...], preferred_element_type=F32)
        o_ref[rows, :] = x_ref[rows, :] + gate_ref[...] * _rms(y, g_ref[...])


def _postmix(x2d, oa2d, ob2d, z2d, wa, wb, wo, g, gate, seq):
    t, d = x2d.shape
    tm = POST_TM
    nsb = seq // tm
    n_gate = d // CHUNK_W
    row = lambda i: (i, 0)
    const2 = lambda i: (0, 0)
    resident = lambda shape: pl.BlockSpec(shape, const2, pipeline_mode=pl.Buffered(1))
    gate_blk = lambda off: pl.BlockSpec((tm, CHUNK_W), functools.partial(
        lambda chunk, i: (i, chunk), _z_chunk(off)))
    gate_specs = ([gate_blk(GA_OFF + c * CHUNK_W) for c in range(n_gate)]
                  + [gate_blk(GB_OFF + c * CHUNK_W) for c in range(n_gate)])
    return pl.pallas_call(
        _postmix_kernel,
        grid=(t // tm,),
        in_specs=[
            pl.BlockSpec((tm, d), row),
            pl.BlockSpec((tm, oa2d.shape[1]), row),
            pl.BlockSpec((tm, ob2d.shape[1]), row),
            *gate_specs,
            resident(wa.shape), resident(wb.shape), resident(wo.shape),
            pl.BlockSpec((1, d), const2),
            pl.BlockSpec((None, 1, d), lambda i: (i // nsb, 0, 0)),
        ],
        out_specs=pl.BlockSpec((tm, d), row),
        out_shape=jax.ShapeDtypeStruct((t, d), F32),
        scratch_shapes=[pltpu.VMEM((tm, d), BF16)],
        compiler_params=_params(("arbitrary",)),
        name="postmix",
    )(x2d, oa2d, ob2d, *([z2d] * (2 * n_gate)), wa, wb, wo, g, gate)


def _ffn_kernel(x_ref, g_ref, sc_ref, sh_ref, w1_ref, w3_ref, w2_ref, gpost_ref, gate_ref, o_ref,
                u_scr):
    k = pl.program_id(1)
    last = pl.num_programs(1) - 1

    def chunk(first):
        u = u_scr[...]
        h1 = jnp.dot(u, w1_ref[...], preferred_element_type=F32)
        h3 = jnp.dot(u, w3_ref[...], preferred_element_type=F32)
        h = (h1 * _sigmoid(h1) * h3).astype(BF16)
        part = jnp.dot(h, w2_ref[...], preferred_element_type=F32)
        if first:
            o_ref[...] = part
        else:
            o_ref[...] += part

    @pl.when(k == 0)
    def _():
        u_scr[...] = _mod_norm(x_ref[...], g_ref[...], sc_ref[...], sh_ref[...]).astype(BF16)
        chunk(True)

    pl.when(k > 0)(functools.partial(chunk, False))

    @pl.when(k == last)
    def _():
        o_ref[...] = x_ref[...] + gate_ref[...] * _rms(o_ref[...], gpost_ref[...])


def _ffn(x2d, g_pre, mod_scale, mod_shift, w13, w2, g_post, gate, seq):
    t, d = x2d.shape
    tm, tf = FFN_TM, FFN_TF
    d_ff = w2.shape[0]
    nk = d_ff // tf
    nsb = seq // tm
    row = lambda i, k: (i, 0)
    const2 = lambda i, k: (0, 0)
    per_batch = lambda i, k: (i // nsb, 0, 0)
    return pl.pallas_call(
        _ffn_kernel,
        grid=(t // tm, nk),
        in_specs=[
            pl.BlockSpec((tm, d), row, pipeline_mode=pl.Buffered(1)),
            pl.BlockSpec((1, d), const2),
            pl.BlockSpec((None, 1, d), per_batch),
            pl.BlockSpec((None, 1, d), per_batch),
            pl.BlockSpec((d, tf), lambda i, k: (0, k)),
            pl.BlockSpec((d, tf), lambda i, k: (0, nk + k)),
            pl.BlockSpec((tf, d), lambda i, k: (k, 0)),
            pl.BlockSpec((1, d), const2),
            pl.BlockSpec((None, 1, d), per_batch),
        ],
        out_specs=pl.BlockSpec((tm, d), row),
        out_shape=jax.ShapeDtypeStruct((t, d), F32),
        scratch_shapes=[pltpu.VMEM((tm, d), BF16)],
        compiler_params=_params(("arbitrary", "arbitrary")),
        name="ffn",
    )(x2d, g_pre, mod_scale, mod_shift, w13, w13, w2, g_post, gate)


def _rope_tables(seq):
    def angles(pos, dim):
        inv_freq = ROPE_THETA ** (-jnp.arange(0, dim, 2, dtype=F32) / dim)
        ang = pos.astype(F32)[:, None] * inv_freq[None, :]
        return jnp.cos(ang), jnp.sin(ang)

    t = jnp.arange(seq)
    c1, s1 = angles(t, HEAD_DIM)
    cr, sr = angles(t // GRID_W, HEAD_DIM // 2)
    cc, sc = angles(t % GRID_W, HEAD_DIM // 2)
    zero = jnp.zeros_like(sr)
    cos1 = jnp.concatenate([c1, c1], axis=-1)
    sin1 = jnp.concatenate([-s1, s1], axis=-1)
    cosx = jnp.concatenate([cr, cr, cc, cc], axis=-1)
    sin_lo = jnp.concatenate([-sr, zero, -sc, zero], axis=-1)
    sin_hi = jnp.concatenate([zero, sr, zero, sc], axis=-1)
    return cos1, sin1, cosx, sin_lo, sin_hi


def _encoder(x, mods, weights):
    b, seq, d = x.shape
    tables = _rope_tables(seq)
    x2d = x.reshape(b * seq, d)
    for l, wl in enumerate(weights):
        shift_m, scale_m, gate_m, shift_f, scale_f, gate_f = [
            mods[l][:, None, i * d:(i + 1) * d] for i in range(N_MOD)]
        z2d = _inproj(x2d, scale_m, shift_m, wl["g_pre_mix"], wl["w_in"], tables, wl["q_norm_b"],
                      wl["k_norm_b"], seq)
        z3 = z2d.reshape(b, seq, IN_W)
        oa = _window_attention(z3, wl["sink_rows"])
        ob = _dense_attention(z3)
        x2d = _postmix(x2d, oa.reshape(b * seq, -1), ob.reshape(b * seq, -1), z2d, wl["w_branch_a"],
                       wl["w_branch_b"], wl["w_out"], wl["g_post_mix"], gate_m, seq)
        x2d = _ffn(x2d, wl["g_pre_ffn"], scale_f, shift_f, wl["w_13"], wl["w_2"], wl["g_post_ffn"],
                   gate_f, seq)
    return x2d.reshape(b, seq, d)


def kernel(x_prompt, x_sample, c_prompt, c_sample, g_pre_mix, g_post_mix, g_pre_ffn, g_post_ffn, w_mod,
           b_mod, w_in, q_norm_b, k_norm_b, sink_a, w_branch_a, w_branch_b, w_out, w_13, w_2):
    depth = w_in.shape[0]
    nb_p, nb_s = c_prompt.shape[0], c_sample.shape[0]
    assert nb_p + nb_s <= MOD_ROWS
    c_all = jnp.zeros((MOD_ROWS, D_MODEL), F32)
    c_all = c_all.at[:nb_p].set(c_prompt).at[nb_p:nb_p + nb_s].set(c_sample)
    mod = _modulation(c_all, w_mod, b_mod)
    mods_p = [mod[l, :nb_p] for l in range(depth)]
    mods_s = [mod[l, nb_p:nb_p + nb_s] for l in range(depth)]

    weights = []
    for l in range(depth):
        sink_rows = jnp.broadcast_to(
            sink_a[l].astype(F32).reshape(A_KV_HEADS, 1, GROUP, 1),
            (A_KV_HEADS, 1, GROUP, WINDOW)).reshape(A_KV_HEADS, 1, GROUP * WINDOW)
        weights.append({
            "g_pre_mix": g_pre_mix[l][None], "g_post_mix": g_post_mix[l][None],
            "g_pre_ffn": g_pre_ffn[l][None], "g_post_ffn": g_post_ffn[l][None],
            "q_norm_b": q_norm_b[l][None], "k_norm_b": k_norm_b[l][None],
            "sink_rows": sink_rows,
            "w_in": jnp.concatenate([w_in[l][:, off:off + CHUNK_W] for off, _ in Z_CHUNKS],
                                    axis=1).astype(BF16),
            "w_branch_a": w_branch_a[l].astype(BF16), "w_branch_b": w_branch_b[l].astype(BF16),
            "w_out": w_out[l].astype(BF16),
            "w_13": w_13[l].astype(BF16), "w_2": w_2[l].astype(BF16),
        })

    y_prompt = _encoder(x_prompt, mods_p, weights)
    y_sample = _encoder(x_sample, mods_s, weights)
    return (y_prompt, y_sample)
```

```python
import functools

import jax
import jax.numpy as jnp
from jax import lax
from jax.experimental import pallas as pl
from jax.experimental.pallas import tpu as pltpu

D_MODEL = 2048
HEAD_DIM = 128
A_Q_HEADS = 8
A_KV_HEADS = 2
B_Q_HEADS = 8
B_KV_HEADS = 2
GROUP = A_Q_HEADS // A_KV_HEADS
WINDOW = 128
GRID_W = 64
ROPE_THETA = 10000.0
N_MOD = 6
EPS = 1e-6
MASK_VALUE = -1e30
LOG2_E = 1.4426950408889634

QA_OFF = 0
KA_OFF = QA_OFF + A_Q_HEADS * HEAD_DIM
VA_OFF = KA_OFF + A_KV_HEADS * HEAD_DIM
QB_OFF = VA_OFF + A_KV_HEADS * HEAD_DIM
KB_OFF = QB_OFF + B_Q_HEADS * HEAD_DIM
VB_OFF = KB_OFF + B_KV_HEADS * HEAD_DIM
GA_OFF = VB_OFF + B_KV_HEADS * HEAD_DIM
GB_OFF = GA_OFF + D_MODEL
IN_W = GB_OFF + D_MODEL

CHUNK_W = GROUP * HEAD_DIM
Z_CHUNKS = (
    (QA_OFF, "qa"), (GA_OFF, "gate"),
    (QA_OFF + CHUNK_W, "qa"), (GA_OFF + CHUNK_W, "gate"),
    (KA_OFF, "kva"), (GA_OFF + 2 * CHUNK_W, "gate"),
    (QB_OFF, "qb"), (GA_OFF + 3 * CHUNK_W, "gate"),
    (QB_OFF + CHUNK_W, "qb"), (GB_OFF, "gate"),
    (KB_OFF, "kvb"), (GB_OFF + CHUNK_W, "gate"),
    (GB_OFF + 2 * CHUNK_W, "gate"), (GB_OFF + 3 * CHUNK_W, "gate"),
)
assert len(Z_CHUNKS) * CHUNK_W == IN_W


def _z_chunk(src_off):
    return [off for off, _ in Z_CHUNKS].index(src_off)

V7X_VMEM_BYTES = 64 * 1024 * 1024
VMEM_LIMIT = V7X_VMEM_BYTES - 8 * 1024 * 1024

F32 = jnp.float32
BF16 = jnp.bfloat16

MOD_ROWS = 16
MOD_TN = 1024
INPROJ_TM = 1024
INPROJ_TN = 1024
INPROJ_ROWS = 512
WIN_TQ = 1024
DENSE_TQ = 256
DENSE_TK = 512
DENSE_ONES_ROWS = 16
DENSE_UNROLL_SHORT = 2
DENSE_UNROLL_LONG = 4
DENSE_LONG_CHUNKS = 16
POST_TM = 512
POST_ROWS = 256
FFN_TM = 512
FFN_TF = 512


def _params(sem):
    return pltpu.CompilerParams(dimension_semantics=sem, vmem_limit_bytes=VMEM_LIMIT)


def _mod_kernel(c_ref, w_ref, b_ref, o_ref):
    c = c_ref[...]
    a = (c * jax.nn.sigmoid(c)).astype(BF16)
    w = w_ref[...].astype(BF16)
    o_ref[...] = jnp.dot(a, w, preferred_element_type=F32) + b_ref[...]


def _modulation(c_all, w_mod, b_mod):
    depth, d, n = w_mod.shape
    return pl.pallas_call(
        _mod_kernel,
        grid=(depth, n // MOD_TN),
        in_specs=[
            pl.BlockSpec((MOD_ROWS, d), lambda l, j: (0, 0)),
            pl.BlockSpec((None, d, MOD_TN), lambda l, j: (l, 0, j)),
            pl.BlockSpec((None, 1, MOD_TN), lambda l, j: (l, 0, j)),
        ],
        out_specs=pl.BlockSpec((None, MOD_ROWS, MOD_TN), lambda l, j: (l, 0, j)),
        out_shape=jax.ShapeDtypeStruct((depth, MOD_ROWS, n), F32),
        compiler_params=_params(("arbitrary", "arbitrary")),
        name="modulation",
    )(c_all, w_mod, b_mod.reshape(depth, 1, n))


def _mod_norm(x, g, scale, shift):
    ms = jnp.mean(x * x, axis=-1, keepdims=True)
    y = x * lax.rsqrt(ms + EPS) * g
    return y * (1.0 + scale) + shift


def _sigmoid(x):
    return 0.5 * jnp.tanh(0.5 * x) + 0.5


def _rms(y, g):
    ms = jnp.mean(y * y, axis=-1, keepdims=True)
    return y * lax.rsqrt(ms + EPS) * g


def _rope1d(h, cos, sin_signed):
    return h * cos + pltpu.roll(h, HEAD_DIM // 2, 1) * sin_signed


def _rope_axial(h, cos, sin_lo, sin_hi):
    q = HEAD_DIM // 4
    return h * cos + pltpu.roll(h, HEAD_DIM - q, 1) * sin_lo + pltpu.roll(h, q, 1) * sin_hi


def _inproj_kernel(x_ref, g_ref, sc_ref, sh_ref, w_ref, cos1_ref, sin1_ref, cosx_ref, sinlo_ref,
                   sinhi_ref, qn_ref, kn_ref, z_ref, u_scr):
    j = pl.program_id(1)
    qscale = HEAD_DIM ** -0.5 * LOG2_E

    def rope_a(v, rows, scale):
        r = _rope1d(v, cos1_ref[rows, :], sin1_ref[rows, :])
        return r * scale if scale != 1.0 else r

    def rope_b(v, rows, gain):
        ms = jnp.mean(v * v, axis=-1, keepdims=True)
        vn = v * lax.rsqrt(ms + EPS) * gain
        return _rope_axial(vn, cosx_ref[rows, :], sinlo_ref[rows, :], sinhi_ref[rows, :])

    q_a = lambda v, rows: rope_a(v, rows, qscale)
    k_a = lambda v, rows: rope_a(v, rows, 1.0)
    q_b = lambda v, rows: rope_b(v, rows, qn_ref[...] * qscale)
    k_b = lambda v, rows: rope_b(v, rows, kn_ref[...])
    keep = lambda v, rows: v
    head_ops = {
        "qa": [q_a] * GROUP,
        "kva": [k_a] * A_KV_HEADS + [keep] * A_KV_HEADS,
        "qb": [q_b] * GROUP,
        "kvb": [k_b] * B_KV_HEADS + [keep] * B_KV_HEADS,
    }
    chunks_per_block = INPROJ_TN // CHUNK_W
    tm = u_scr.shape[0]

    def run(kinds):
        for ci, kind in enumerate(kinds):
            cols = slice(ci * CHUNK_W, (ci + 1) * CHUNK_W)
            for r0 in range(0, tm, INPROJ_ROWS):
                rows = slice(r0, r0 + INPROJ_ROWS)
                acc = jnp.dot(u_scr[rows, :], w_ref[:, cols], preferred_element_type=F32)
                if kind == "gate":
                    z_ref[rows, cols] = _sigmoid(acc).astype(BF16)
                    continue
                for hh, op in enumerate(head_ops[kind]):
                    lo = ci * CHUNK_W + hh * HEAD_DIM
                    v = op(acc[:, hh * HEAD_DIM:(hh + 1) * HEAD_DIM], rows)
                    z_ref[rows, lo:lo + HEAD_DIM] = v.astype(BF16)

    n_blocks = len(Z_CHUNKS) // chunks_per_block
    block_kinds = [tuple(k for _, k in Z_CHUNKS[bj * chunks_per_block:(bj + 1) * chunks_per_block])
                   for bj in range(n_blocks)]
    @pl.when(j == 0)
    def _():
        u_scr[...] = _mod_norm(x_ref[...], g_ref[...], sc_ref[...], sh_ref[...]).astype(BF16)
        run(block_kinds[0])

    for kinds in dict.fromkeys(block_kinds[1:]):
        cond = functools.reduce(jnp.logical_or, [j == bj for bj in range(1, n_blocks)
                                                 if block_kinds[bj] == kinds])
        pl.when(cond)(functools.partial(run, kinds))


MOD_SHIFT_M, MOD_SCALE_M, MOD_GATE_M, MOD_SHIFT_F, MOD_SCALE_F, MOD_GATE_F = range(N_MOD)


def _mod_spec(which, l, row0, rows_per_batch_block):
    return pl.BlockSpec(
        (None, None, None, 1, D_MODEL),
        lambda i, *_: (l, row0 + i // rows_per_batch_block, which, 0, 0))


def _layer_vec_spec(l, width):
    return pl.BlockSpec((None, 1, width), lambda *_: (l, 0, 0))


def _inproj(x2d, mod5, row0, l, g, w_in, tables, qn, kn, seq):
    t, d = x2d.shape
    tm, tn = INPROJ_TM, INPROJ_TN
    nsb = seq // tm
    row = lambda i, j: (i, 0)
    pos = lambda i, j: (i % nsb, 0)
    tab_spec = pl.BlockSpec((tm, HEAD_DIM), pos)
    return pl.pallas_call(
        _inproj_kernel,
        grid=(t // tm, IN_W // tn),
        in_specs=[
            pl.BlockSpec((tm, d), row),
            _layer_vec_spec(l, d),
            _mod_spec(MOD_SCALE_M, l, row0, nsb),
            _mod_spec(MOD_SHIFT_M, l, row0, nsb),
            pl.BlockSpec((None, d, tn), lambda i, j: (l, 0, j)),
            tab_spec, tab_spec, tab_spec, tab_spec, tab_spec,
            _layer_vec_spec(l, HEAD_DIM),
            _layer_vec_spec(l, HEAD_DIM),
        ],
        out_specs=pl.BlockSpec((tm, tn), lambda i, j: (i, j)),
        out_shape=jax.ShapeDtypeStruct((t, IN_W), BF16),
        scratch_shapes=[pltpu.VMEM((tm, d), BF16)],
        compiler_params=_params(("arbitrary", "arbitrary")),
        name="inproj",
    )(x2d, g, mod5, mod5, w_in, *tables, qn, kn)


def _window_kernel(q_ref, kp_ref, km_ref, kn_ref, vp_ref, vm_ref, vn_ref, sink_ref, o_ref, s_scr, *,
                   seq):
    qi = pl.program_id(2)
    tq = q_ref.shape[0]
    kcat = jnp.concatenate([kp_ref[...], km_ref[...], kn_ref[...]], axis=0)
    vcat = jnp.concatenate([vp_ref[...], vm_ref[...], vn_ref[...]], axis=0)
    vt = jnp.concatenate([vcat.astype(F32).T.astype(BF16),
                          jnp.ones((DENSE_ONES_ROWS, tq + 2 * WINDOW), BF16)], axis=0)
    sink = sink_ref[...] * LOG2_E
    lanes = GROUP * WINDOW
    r = lax.broadcasted_iota(jnp.int32, (1, lanes), 1) & (WINDOW - 1)
    kk = lax.broadcasted_iota(jnp.int32, (WINDOW, lanes), 0)
    n_sub = tq // WINDOW

    def masked_scores(sb):
        base = qi * tq + (sb - 1) * WINDOW
        lo = jnp.maximum(r, -base)
        hi = jnp.minimum(r + 2 * WINDOW, seq - 1 - base)
        qb = q_ref[sb * WINDOW:(sb + 1) * WINDOW, :]
        qs = jnp.concatenate([qb[:, g * HEAD_DIM:(g + 1) * HEAD_DIM] for g in range(GROUP)], axis=0)
        kj = kcat[sb * WINDOW:(sb + 3) * WINDOW]
        s = lax.dot_general(kj, qs, (((1,), (1,)), ((), ())), preferred_element_type=F32)
        top = jnp.where(kk >= lo, s[:WINDOW], MASK_VALUE)
        mid = s[WINDOW:2 * WINDOW]
        bot = jnp.where(kk + 2 * WINDOW <= hi, s[2 * WINDOW:], MASK_VALUE)
        s_scr[sb % 2, 0:WINDOW] = top
        s_scr[sb % 2, WINDOW:2 * WINDOW] = mid
        s_scr[sb % 2, 2 * WINDOW:] = bot
        col_max = jnp.maximum(jnp.maximum(jnp.max(top, axis=0, keepdims=True),
                                          jnp.max(mid, axis=0, keepdims=True)),
                              jnp.max(bot, axis=0, keepdims=True))
        return jnp.maximum(col_max, sink)

    def finish(sb, m):
        p = jnp.exp2(s_scr[sb % 2] - m).astype(BF16)
        acc = jnp.dot(vt[:, sb * WINDOW:(sb + 3) * WINDOW], p, preferred_element_type=F32)
        denom = acc[HEAD_DIM:HEAD_DIM + 1] + jnp.exp2(sink - m)
        ot = acc[:HEAD_DIM] / denom
        for g in range(GROUP):
            o_ref[sb * WINDOW:(sb + 1) * WINDOW, g * HEAD_DIM:(g + 1) * HEAD_DIM] = (
                ot[:, g * WINDOW:(g + 1) * WINDOW].T.astype(BF16))

    m = masked_scores(0)
    for sb in range(n_sub):
        m_next = masked_scores(sb + 1) if sb + 1 < n_sub else None
        finish(sb, m)
        m = m_next


def _window_attention(z3, sink_rows, l):
    b, seq, _ = z3.shape
    tq = WIN_TQ
    r = tq // WINDOW
    nblk = seq // WINDOW
    gw = CHUNK_W
    heads_per_chunk = CHUNK_W // HEAD_DIM
    k0 = _z_chunk(KA_OFF) * heads_per_chunk
    v0 = k0 + A_KV_HEADS
    q_chunks = [_z_chunk(QA_OFF + g * CHUNK_W) for g in range(A_KV_HEADS)]
    assert q_chunks[1] - q_chunks[0] == 2
    prev = lambda off: (lambda bi, h, qi: (bi, jnp.maximum(qi * r - 1, 0), off + h))
    main = lambda off: (lambda bi, h, qi: (bi, qi, off + h))
    nxt = lambda off: (lambda bi, h, qi: (bi, jnp.minimum((qi + 1) * r, nblk - 1), off + h))
    edge = lambda f: pl.BlockSpec((None, WINDOW, HEAD_DIM), f)
    body = lambda f: pl.BlockSpec((None, tq, HEAD_DIM), f)
    return pl.pallas_call(
        functools.partial(_window_kernel, seq=seq),
        grid=(b, A_KV_HEADS, seq // tq),
        in_specs=[
            pl.BlockSpec((None, tq, gw), lambda bi, h, qi: (bi, qi, q_chunks[0] + 2 * h)),
            edge(prev(k0)), body(main(k0)), edge(nxt(k0)),
            edge(prev(v0)), body(main(v0)), edge(nxt(v0)),
            pl.BlockSpec((None, None, 1, GROUP * WINDOW), lambda bi, h, qi: (l, h, 0, 0)),
        ],
        out_specs=pl.BlockSpec((None, tq, gw), lambda bi, h, qi: (bi, qi, h)),
        out_shape=jax.ShapeDtypeStruct((b, seq, A_Q_HEADS * HEAD_DIM), BF16),
        scratch_shapes=[pltpu.VMEM((2, 3 * WINDOW, GROUP * WINDOW), F32)],
        compiler_params=_params(("arbitrary", "arbitrary", "arbitrary")),
        name="window_attn",
    )(z3, z3, z3, z3, z3, z3, z3, sink_rows)


def _dense_kernel(q_ref, k_ref, v_ref, o_ref, vt_scr, s0_scr, s1_scr):
    tq = q_ref.shape[0]
    seq = k_ref.shape[0]
    cols = GROUP * tq

    @pl.when(pl.program_id(2) == 0)
    def _():
        for c in range(seq // DENSE_TK):
            blk = v_ref[c * DENSE_TK:(c + 1) * DENSE_TK, :].astype(F32)
            vt_scr[:HEAD_DIM, c * DENSE_TK:(c + 1) * DENSE_TK] = blk.T.astype(BF16)
        vt_scr[HEAD_DIM:, :] = jnp.ones((DENSE_ONES_ROWS, seq), BF16)

    q = jnp.concatenate([q_ref[:, g * HEAD_DIM:(g + 1) * HEAD_DIM] for g in range(GROUP)], axis=0)

    def scores(ci, s_scr):
        start = pl.multiple_of(ci * DENSE_TK, DENSE_TK)
        k = k_ref[pl.ds(start, DENSE_TK), :]
        s = lax.dot_general(k, q, (((1,), (1,)), ((), ())), preferred_element_type=F32)
        s_scr[...] = s
        return jnp.max(s, axis=0, keepdims=True)

    def update(ci, s_scr, mc, state):
        m, acc = state
        start = pl.multiple_of(ci * DENSE_TK, DENSE_TK)
        vt = vt_scr[:, pl.ds(start, DENSE_TK)]
        m_new = jnp.maximum(m, mc)
        alpha = jnp.exp2(m - m_new)
        p = jnp.exp2(s_scr[...] - m_new).astype(BF16)
        acc = alpha * acc + jnp.dot(vt, p, preferred_element_type=F32)
        return m_new, acc

    n_chunks = seq // DENSE_TK
    bufs = (s0_scr, s1_scr)
    unroll = DENSE_UNROLL_LONG if n_chunks >= DENSE_LONG_CHUNKS else DENSE_UNROLL_SHORT
    unroll = min(unroll, n_chunks)
    assert n_chunks % unroll == 0 and (unroll % 2 == 0 or n_chunks == 1)

    def group(base, mc, state, last):
        for j in range(unroll):
            c = base + j
            nxt = None if (last and j == unroll - 1) else scores(c + 1, bufs[(j + 1) % 2])
            state = update(c, bufs[j % 2], mc, state)
            mc = nxt
        return mc, state

    state = (jnp.full((1, cols), -jnp.inf, F32),
             jnp.zeros((HEAD_DIM + DENSE_ONES_ROWS, cols), F32))
    mc, state = lax.fori_loop(
        0, n_chunks // unroll - 1,
        lambda i, carry: group(i * unroll, carry[0], carry[1], False),
        (scores(0, bufs[0]), state))
    _, (_, acc) = group(n_chunks - unroll, mc, state, True)
    o = acc[:HEAD_DIM] / acc[HEAD_DIM:HEAD_DIM + 1]
    for g in range(GROUP):
        o_ref[:, g * HEAD_DIM:(g + 1) * HEAD_DIM] = o[:, g * tq:(g + 1) * tq].T.astype(BF16)


def _dense_attention(z3):
    b, seq, _ = z3.shape
    tq = DENSE_TQ
    gw = CHUNK_W
    heads_per_chunk = CHUNK_W // HEAD_DIM
    k0 = _z_chunk(KB_OFF) * heads_per_chunk
    v0 = k0 + B_KV_HEADS
    q_chunks = [_z_chunk(QB_OFF + g * CHUNK_W) for g in range(B_KV_HEADS)]
    assert q_chunks[1] - q_chunks[0] == 2
    return pl.pallas_call(
        _dense_kernel,
        grid=(b, B_KV_HEADS, seq // tq),
        in_specs=[
            pl.BlockSpec((None, tq, gw), lambda bi, h, qi: (bi, qi, q_chunks[0] + 2 * h)),
            pl.BlockSpec((None, seq, HEAD_DIM), lambda bi, h, qi: (bi, 0, k0 + h)),
            pl.BlockSpec((None, seq, HEAD_DIM), lambda bi, h, qi: (bi, 0, v0 + h)),
        ],
        out_specs=pl.BlockSpec((None, tq, gw), lambda bi, h, qi: (bi, qi, h)),
        out_shape=jax.ShapeDtypeStruct((b, seq, B_Q_HEADS * HEAD_DIM), BF16),
        scratch_shapes=[pltpu.VMEM((HEAD_DIM + DENSE_ONES_ROWS, seq), BF16),
                        pltpu.VMEM((DENSE_TK, GROUP * tq), F32),
                        pltpu.VMEM((DENSE_TK, GROUP * tq), F32)],
        compiler_params=_params(("arbitrary", "arbitrary", "arbitrary")),
        name="dense_attn",
    )(z3, z3, z3)


def _postmix_kernel(x_ref, oa_ref, ob_ref, *rest):
    n_gate = D_MODEL // CHUNK_W
    ga_refs, gb_refs = rest[:n_gate], rest[n_gate:2 * n_gate]
    wa_ref, wb_ref, wo_ref, g_ref, gate_ref, o_ref, m_scr = rest[2 * n_gate:]
    for r0 in range(0, x_ref.shape[0], POST_ROWS):
        rows = slice(r0, r0 + POST_ROWS)
        oa = oa_ref[rows, :]
        ob = ob_ref[rows, :]
        for ci, (ga_ref, gb_ref) in enumerate(zip(ga_refs, gb_refs)):
            cs = slice(ci * CHUNK_W, (ci + 1) * CHUNK_W)
            pa = jnp.dot(oa, wa_ref[:, cs], preferred_element_type=F32)
            pb = jnp.dot(ob, wb_ref[:, cs], preferred_element_type=F32)
            merged = ga_ref[rows, :].astype(F32) * pa + gb_ref[rows, :].astype(F32) * pb
            m_scr[rows, cs] = merged.astype(BF16)
        y = jnp.dot(m_scr[rows, :], wo_ref[...], preferred_element_type=F32)
        o_ref[rows, :] = x_ref[rows, :] + gate_ref[...] * _rms(y, g_ref[...])


def _postmix(x2d, oa2d, ob2d, z2d, wa, wb, wo, g, mod5, row0, l, seq):
    t, d = x2d.shape
    tm = POST_TM
    nsb = seq // tm
    n_gate = d // CHUNK_W
    row = lambda i: (i, 0)
    resident = lambda shape: pl.BlockSpec((None,) + tuple(shape[1:]), lambda i: (l, 0, 0),
                                          pipeline_mode=pl.Buffered(1))
    gate_blk = lambda off: pl.BlockSpec((tm, CHUNK_W), functools.partial(
        lambda chunk, i: (i, chunk), _z_chunk(off)))
    gate_specs = ([gate_blk(GA_OFF + c * CHUNK_W) for c in range(n_gate)]
                  + [gate_blk(GB_OFF + c * CHUNK_W) for c in range(n_gate)])
    return pl.pallas_call(
        _postmix_kernel,
        grid=(t // tm,),
        in_specs=[
            pl.BlockSpec((tm, d), row),
            pl.BlockSpec((tm, oa2d.shape[1]), row),
            pl.BlockSpec((tm, ob2d.shape[1]), row),
            *gate_specs,
            resident(wa.shape), resident(wb.shape), resident(wo.shape),
            _layer_vec_spec(l, d),
            _mod_spec(MOD_GATE_M, l, row0, nsb),
        ],
        out_specs=pl.BlockSpec((tm, d), row),
        out_shape=jax.ShapeDtypeStruct((t, d), F32),
        scratch_shapes=[pltpu.VMEM((tm, d), BF16)],
        compiler_params=_params(("arbitrary",)),
        name="postmix",
    )(x2d, oa2d, ob2d, *([z2d] * (2 * n_gate)), wa, wb, wo, g, mod5)


def _ffn_kernel(x_ref, g_ref, sc_ref, sh_ref, w1_ref, w3_ref, w2_ref, gpost_ref, gate_ref, o_ref,
                u_scr):
    k = pl.program_id(1)
    last = pl.num_programs(1) - 1

    def chunk(first):
        u = u_scr[...]
        h1 = jnp.dot(u, w1_ref[...], preferred_element_type=F32)
        h3 = jnp.dot(u, w3_ref[...], preferred_element_type=F32)
        h = (h1 * _sigmoid(h1) * h3).astype(BF16)
        part = jnp.dot(h, w2_ref[...], preferred_element_type=F32)
        if first:
            o_ref[...] = part
        else:
            o_ref[...] += part

    @pl.when(k == 0)
    def _():
        u_scr[...] = _mod_norm(x_ref[...], g_ref[...], sc_ref[...], sh_ref[...]).astype(BF16)
        chunk(True)

    pl.when(k > 0)(functools.partial(chunk, False))

    @pl.when(k == last)
    def _():
        o_ref[...] = x_ref[...] + gate_ref[...] * _rms(o_ref[...], gpost_ref[...])


def _ffn(x2d, g_pre, mod5, row0, l, w13, w2, g_post, seq):
    t, d = x2d.shape
    tm, tf = FFN_TM, FFN_TF
    d_ff = w2.shape[1]
    nk = d_ff // tf
    nsb = seq // tm
    row = lambda i, k: (i, 0)
    return pl.pallas_call(
        _ffn_kernel,
        grid=(t // tm, nk),
        in_specs=[
            pl.BlockSpec((tm, d), row),
            _layer_vec_spec(l, d),
            _mod_spec(MOD_SCALE_F, l, row0, nsb),
            _mod_spec(MOD_SHIFT_F, l, row0, nsb),
            pl.BlockSpec((None, d, tf), lambda i, k: (l, 0, k)),
            pl.BlockSpec((None, d, tf), lambda i, k: (l, 0, nk + k)),
            pl.BlockSpec((None, tf, d), lambda i, k: (l, k, 0)),
            _layer_vec_spec(l, d),
            _mod_spec(MOD_GATE_F, l, row0, nsb),
        ],
        out_specs=pl.BlockSpec((tm, d), row),
        out_shape=jax.ShapeDtypeStruct((t, d), F32),
        scratch_shapes=[pltpu.VMEM((tm, d), BF16)],
        compiler_params=_params(("arbitrary", "arbitrary")),
        name="ffn",
    )(x2d, g_pre, mod5, mod5, w13, w13, w2, g_post, mod5)


def _rope_tables(seq):
    def angles(pos, dim):
        inv_freq = ROPE_THETA ** (-jnp.arange(0, dim, 2, dtype=F32) / dim)
        ang = pos.astype(F32)[:, None] * inv_freq[None, :]
        return jnp.cos(ang), jnp.sin(ang)

    t = jnp.arange(seq)
    c1, s1 = angles(t, HEAD_DIM)
    cr, sr = angles(t // GRID_W, HEAD_DIM // 2)
    cc, sc = angles(t % GRID_W, HEAD_DIM // 2)
    zero = jnp.zeros_like(sr)
    cos1 = jnp.concatenate([c1, c1], axis=-1)
    sin1 = jnp.concatenate([-s1, s1], axis=-1)
    cosx = jnp.concatenate([cr, cr, cc, cc], axis=-1)
    sin_lo = jnp.concatenate([-sr, zero, -sc, zero], axis=-1)
    sin_hi = jnp.concatenate([zero, sr, zero, sc], axis=-1)
    return cos1, sin1, cosx, sin_lo, sin_hi


def _encoder(x, mod5, row0, w):
    b, seq, d = x.shape
    tables = _rope_tables(seq)
    x2d = x.reshape(b * seq, d)
    for l in range(w["w_in"].shape[0]):
        z2d = _inproj(x2d, mod5, row0, l, w["g_pre_mix"], w["w_in"], tables, w["q_norm_b"],
                      w["k_norm_b"], seq)
        z3 = z2d.reshape(b, seq, IN_W)
        oa = _window_attention(z3, w["sink_rows"], l)
        ob = _dense_attention(z3)
        x2d = _postmix(x2d, oa.reshape(b * seq, -1), ob.reshape(b * seq, -1), z2d, w["w_branch_a"],
                       w["w_branch_b"], w["w_out"], w["g_post_mix"], mod5, row0, l, seq)
        x2d = _ffn(x2d, w["g_pre_ffn"], mod5, row0, l, w["w_13"], w["w_2"], w["g_post_ffn"], seq)
    return x2d.reshape(b, seq, d)


def kernel(x_prompt, x_sample, c_prompt, c_sample, g_pre_mix, g_post_mix, g_pre_ffn, g_post_ffn, w_mod,
           b_mod, w_in, q_norm_b, k_norm_b, sink_a, w_branch_a, w_branch_b, w_out, w_13, w_2):
    depth = w_in.shape[0]
    nb_p, nb_s = c_prompt.shape[0], c_sample.shape[0]
    assert nb_p + nb_s <= MOD_ROWS
    c_all = jnp.zeros((MOD_ROWS, D_MODEL), F32)
    c_all = c_all.at[:nb_p].set(c_prompt).at[nb_p:nb_p + nb_s].set(c_sample)
    mod = _modulation(c_all, w_mod, b_mod)
    mod5 = mod.reshape(depth, MOD_ROWS, N_MOD, 1, D_MODEL)

    vec = lambda a: a.reshape(depth, 1, a.shape[-1])
    weights = {
        "g_pre_mix": vec(g_pre_mix), "g_post_mix": vec(g_post_mix),
        "g_pre_ffn": vec(g_pre_ffn), "g_post_ffn": vec(g_post_ffn),
        "q_norm_b": vec(q_norm_b), "k_norm_b": vec(k_norm_b),
        "sink_rows": jnp.broadcast_to(
            sink_a.astype(F32).reshape(depth, A_KV_HEADS, 1, GROUP, 1),
            (depth, A_KV_HEADS, 1, GROUP, WINDOW)).reshape(depth, A_KV_HEADS, 1, GROUP * WINDOW),
        "w_in": jnp.concatenate([w_in[:, :, off:off + CHUNK_W] for off, _ in Z_CHUNKS],
                                axis=2).astype(BF16),
        "w_branch_a": w_branch_a.astype(BF16), "w_branch_b": w_branch_b.astype(BF16),
        "w_out": w_out.astype(BF16),
        "w_13": w_13.astype(BF16), "w_2": w_2.astype(BF16),
    }

    y_prompt = _encoder(x_prompt, mod5, 0, weights)
    y_sample = _encoder(x_sample, mod5, nb_p, weights)
    return (y_prompt, y_sample)
```

```python
import functools

import jax
import jax.numpy as jnp
from jax import lax
from jax.experimental import pallas as pl
from jax.experimental.pallas import tpu as pltpu

D_MODEL = 2048
HEAD_DIM = 128
A_Q_HEADS = 8
A_KV_HEADS = 2
B_Q_HEADS = 8
B_KV_HEADS = 2
GROUP = A_Q_HEADS // A_KV_HEADS
WINDOW = 128
GRID_W = 64
ROPE_THETA = 10000.0
N_MOD = 6
EPS = 1e-6
MASK_VALUE = -1e30
LOG2_E = 1.4426950408889634

QA_OFF = 0
KA_OFF = QA_OFF + A_Q_HEADS * HEAD_DIM
VA_OFF = KA_OFF + A_KV_HEADS * HEAD_DIM
QB_OFF = VA_OFF + A_KV_HEADS * HEAD_DIM
KB_OFF = QB_OFF + B_Q_HEADS * HEAD_DIM
VB_OFF = KB_OFF + B_KV_HEADS * HEAD_DIM
GA_OFF = VB_OFF + B_KV_HEADS * HEAD_DIM
GB_OFF = GA_OFF + D_MODEL
IN_W = GB_OFF + D_MODEL

CHUNK_W = GROUP * HEAD_DIM
Z_CHUNKS = (
    (QA_OFF, "qa"), (GA_OFF, "gate"),
    (QA_OFF + CHUNK_W, "qa"), (GA_OFF + CHUNK_W, "gate"),
    (KA_OFF, "kva"), (GA_OFF + 2 * CHUNK_W, "gate"),
    (QB_OFF, "qb"), (GA_OFF + 3 * CHUNK_W, "gate"),
    (QB_OFF + CHUNK_W, "qb"), (GB_OFF, "gate"),
    (KB_OFF, "kvb"), (GB_OFF + CHUNK_W, "gate"),
    (GB_OFF + 2 * CHUNK_W, "gate"), (GB_OFF + 3 * CHUNK_W, "gate"),
)
assert len(Z_CHUNKS) * CHUNK_W == IN_W


def _z_chunk(src_off):
    return [off for off, _ in Z_CHUNKS].index(src_off)

V7X_VMEM_BYTES = 64 * 1024 * 1024
VMEM_LIMIT = V7X_VMEM_BYTES - 8 * 1024 * 1024

F32 = jnp.float32
BF16 = jnp.bfloat16

MOD_ROWS = 16
MOD_TN = 1024
INPROJ_TM = 1024
INPROJ_TN = 1024
INPROJ_ROWS = 512
WIN_TQ = 1024
DENSE_TQ = 256
DENSE_TK = 512
DENSE_ONES_ROWS = 16
DENSE_UNROLL_SHORT = 2
DENSE_UNROLL_LONG = 4
DENSE_LONG_CHUNKS = 16
POST_TM = 512
POST_ROWS = 256
FFN_TM = 1024
FFN_TF = 256
FFN_TAIL_ROWS = 256


def _params(sem):
    return pltpu.CompilerParams(dimension_semantics=sem, vmem_limit_bytes=VMEM_LIMIT)


def _mod_kernel(c_ref, w_ref, b_ref, o_ref):
    c = c_ref[...]
    a = (c * jax.nn.sigmoid(c)).astype(BF16)
    w = w_ref[...].astype(BF16)
    o_ref[...] = jnp.dot(a, w, preferred_element_type=F32) + b_ref[...]


def _modulation(c_all, w_mod, b_mod):
    depth, d, n = w_mod.shape
    return pl.pallas_call(
        _mod_kernel,
        grid=(depth, n // MOD_TN),
        in_specs=[
            pl.BlockSpec((MOD_ROWS, d), lambda l, j: (0, 0)),
            pl.BlockSpec((None, d, MOD_TN), lambda l, j: (l, 0, j)),
            pl.BlockSpec((None, 1, MOD_TN), lambda l, j: (l, 0, j)),
        ],
        out_specs=pl.BlockSpec((None, MOD_ROWS, MOD_TN), lambda l, j: (l, 0, j)),
        out_shape=jax.ShapeDtypeStruct((depth, MOD_ROWS, n), F32),
        compiler_params=_params(("arbitrary", "arbitrary")),
        name="modulation",
    )(c_all, w_mod, b_mod.reshape(depth, 1, n))


def _mod_norm(x, g, scale, shift):
    ms = jnp.mean(x * x, axis=-1, keepdims=True)
    y = x * lax.rsqrt(ms + EPS) * g
    return y * (1.0 + scale) + shift


def _sigmoid(x):
    return 0.5 * jnp.tanh(0.5 * x) + 0.5


def _rms(y, g):
    ms = jnp.mean(y * y, axis=-1, keepdims=True)
    return y * lax.rsqrt(ms + EPS) * g


def _rope1d(h, cos, sin_signed):
    return h * cos + pltpu.roll(h, HEAD_DIM // 2, 1) * sin_signed


def _rope_axial(h, cos, sin_lo, sin_hi):
    q = HEAD_DIM // 4
    return h * cos + pltpu.roll(h, HEAD_DIM - q, 1) * sin_lo + pltpu.roll(h, q, 1) * sin_hi


def _inproj_kernel(x_ref, g_ref, sc_ref, sh_ref, w_ref, cos1_ref, sin1_ref, cosx_ref, sinlo_ref,
                   sinhi_ref, qn_ref, kn_ref, z_ref, u_scr):
    j = pl.program_id(1)
    qscale = HEAD_DIM ** -0.5 * LOG2_E

    def rope_a(v, rows, scale):
        r = _rope1d(v, cos1_ref[rows, :], sin1_ref[rows, :])
        return r * scale if scale != 1.0 else r

    def rope_b(v, rows, gain):
        ms = jnp.mean(v * v, axis=-1, keepdims=True)
        vn = v * lax.rsqrt(ms + EPS) * gain
        return _rope_axial(vn, cosx_ref[rows, :], sinlo_ref[rows, :], sinhi_ref[rows, :])

    q_a = lambda v, rows: rope_a(v, rows, qscale)
    k_a = lambda v, rows: rope_a(v, rows, 1.0)
    q_b = lambda v, rows: rope_b(v, rows, qn_ref[...] * qscale)
    k_b = lambda v, rows: rope_b(v, rows, kn_ref[...])
    keep = lambda v, rows: v
    head_ops = {
        "qa": [q_a] * GROUP,
        "kva": [k_a] * A_KV_HEADS + [keep] * A_KV_HEADS,
        "qb": [q_b] * GROUP,
        "kvb": [k_b] * B_KV_HEADS + [keep] * B_KV_HEADS,
    }
    chunks_per_block = INPROJ_TN // CHUNK_W
    tm = u_scr.shape[0]

    def run(kinds):
        for ci, kind in enumerate(kinds):
            cols = slice(ci * CHUNK_W, (ci + 1) * CHUNK_W)
            for r0 in range(0, tm, INPROJ_ROWS):
                rows = slice(r0, r0 + INPROJ_ROWS)
                acc = jnp.dot(u_scr[rows, :], w_ref[:, cols], preferred_element_type=F32)
                if kind == "gate":
                    z_ref[rows, cols] = _sigmoid(acc).astype(BF16)
                    continue
                for hh, op in enumerate(head_ops[kind]):
                    lo = ci * CHUNK_W + hh * HEAD_DIM
                    v = op(acc[:, hh * HEAD_DIM:(hh + 1) * HEAD_DIM], rows)
                    z_ref[rows, lo:lo + HEAD_DIM] = v.astype(BF16)

    n_blocks = len(Z_CHUNKS) // chunks_per_block
    block_kinds = [tuple(k for _, k in Z_CHUNKS[bj * chunks_per_block:(bj + 1) * chunks_per_block])
                   for bj in range(n_blocks)]
    @pl.when(j == 0)
    def _():
        u_scr[...] = _mod_norm(x_ref[...], g_ref[...], sc_ref[...], sh_ref[...]).astype(BF16)
        run(block_kinds[0])

    for kinds in dict.fromkeys(block_kinds[1:]):
        cond = functools.reduce(jnp.logical_or, [j == bj for bj in range(1, n_blocks)
                                                 if block_kinds[bj] == kinds])
        pl.when(cond)(functools.partial(run, kinds))


MOD_SHIFT_M, MOD_SCALE_M, MOD_GATE_M, MOD_SHIFT_F, MOD_SCALE_F, MOD_GATE_F = range(N_MOD)


def _mod_spec(which, l, row0, rows_per_batch_block):
    return pl.BlockSpec(
        (None, None, None, 1, D_MODEL),
        lambda i, *_: (l, row0 + i // rows_per_batch_block, which, 0, 0))


def _layer_vec_spec(l, width):
    return pl.BlockSpec((None, 1, width), lambda *_: (l, 0, 0))


def _inproj(x2d, mod5, row0, l, g, w_in, tables, qn, kn, seq):
    t, d = x2d.shape
    tm, tn = INPROJ_TM, INPROJ_TN
    nsb = seq // tm
    row = lambda i, j: (i, 0)
    pos = lambda i, j: (i % nsb, 0)
    tab_spec = pl.BlockSpec((tm, HEAD_DIM), pos)
    return pl.pallas_call(
        _inproj_kernel,
        grid=(t // tm, IN_W // tn),
        in_specs=[
            pl.BlockSpec((tm, d), row),
            _layer_vec_spec(l, d),
            _mod_spec(MOD_SCALE_M, l, row0, nsb),
            _mod_spec(MOD_SHIFT_M, l, row0, nsb),
            pl.BlockSpec((None, d, tn), lambda i, j: (l, 0, j)),
            tab_spec, tab_spec, tab_spec, tab_spec, tab_spec,
            _layer_vec_spec(l, HEAD_DIM),
            _layer_vec_spec(l, HEAD_DIM),
        ],
        out_specs=pl.BlockSpec((tm, tn), lambda i, j: (i, j)),
        out_shape=jax.ShapeDtypeStruct((t, IN_W), BF16),
        scratch_shapes=[pltpu.VMEM((tm, d), BF16)],
        compiler_params=_params(("arbitrary", "arbitrary")),
        name="inproj",
    )(x2d, g, mod5, mod5, w_in, *tables, qn, kn)


def _window_kernel(q_ref, kp_ref, km_ref, kn_ref, vp_ref, vm_ref, vn_ref, sink_ref, o_ref, s_scr, *,
                   seq):
    qi = pl.program_id(2)
    tq = q_ref.shape[0]
    kcat = jnp.concatenate([kp_ref[...], km_ref[...], kn_ref[...]], axis=0)
    vcat = jnp.concatenate([vp_ref[...], vm_ref[...], vn_ref[...]], axis=0)
    vt = jnp.concatenate([vcat.astype(F32).T.astype(BF16),
                          jnp.ones((DENSE_ONES_ROWS, tq + 2 * WINDOW), BF16)], axis=0)
    sink = sink_ref[...] * LOG2_E
    lanes = GROUP * WINDOW
    r = lax.broadcasted_iota(jnp.int32, (1, lanes), 1) & (WINDOW - 1)
    kk = lax.broadcasted_iota(jnp.int32, (WINDOW, lanes), 0)
    n_sub = tq // WINDOW

    def masked_scores(sb):
        base = qi * tq + (sb - 1) * WINDOW
        lo = jnp.maximum(r, -base)
        hi = jnp.minimum(r + 2 * WINDOW, seq - 1 - base)
        qb = q_ref[sb * WINDOW:(sb + 1) * WINDOW, :]
        qs = jnp.concatenate([qb[:, g * HEAD_DIM:(g + 1) * HEAD_DIM] for g in range(GROUP)], axis=0)
        kj = kcat[sb * WINDOW:(sb + 3) * WINDOW]
        s = lax.dot_general(kj, qs, (((1,), (1,)), ((), ())), preferred_element_type=F32)
        top = jnp.where(kk >= lo, s[:WINDOW], MASK_VALUE)
        mid = s[WINDOW:2 * WINDOW]
        bot = jnp.where(kk + 2 * WINDOW <= hi, s[2 * WINDOW:], MASK_VALUE)
        s_scr[sb % 2, 0:WINDOW] = top
        s_scr[sb % 2, WINDOW:2 * WINDOW] = mid
        s_scr[sb % 2, 2 * WINDOW:] = bot
        col_max = jnp.maximum(jnp.maximum(jnp.max(top, axis=0, keepdims=True),
                                          jnp.max(mid, axis=0, keepdims=True)),
                              jnp.max(bot, axis=0, keepdims=True))
        return jnp.maximum(col_max, sink)

    def finish(sb, m):
        p = jnp.exp2(s_scr[sb % 2] - m).astype(BF16)
        acc = jnp.dot(vt[:, sb * WINDOW:(sb + 3) * WINDOW], p, preferred_element_type=F32)
        denom = acc[HEAD_DIM:HEAD_DIM + 1] + jnp.exp2(sink - m)
        ot = acc[:HEAD_DIM] / denom
        for g in range(GROUP):
            o_ref[sb * WINDOW:(sb + 1) * WINDOW, g * HEAD_DIM:(g + 1) * HEAD_DIM] = (
                ot[:, g * WINDOW:(g + 1) * WINDOW].T.astype(BF16))

    m = masked_scores(0)
    for sb in range(n_sub):
        m_next = masked_scores(sb + 1) if sb + 1 < n_sub else None
        finish(sb, m)
        m = m_next


def _window_attention(z3, sink_rows, l):
    b, seq, _ = z3.shape
    tq = WIN_TQ
    r = tq // WINDOW
    nblk = seq // WINDOW
    gw = CHUNK_W
    heads_per_chunk = CHUNK_W // HEAD_DIM
    k0 = _z_chunk(KA_OFF) * heads_per_chunk
    v0 = k0 + A_KV_HEADS
    q_chunks = [_z_chunk(QA_OFF + g * CHUNK_W) for g in range(A_KV_HEADS)]
    assert q_chunks[1] - q_chunks[0] == 2
    prev = lambda off: (lambda bi, h, qi: (bi, jnp.maximum(qi * r - 1, 0), off + h))
    main = lambda off: (lambda bi, h, qi: (bi, qi, off + h))
    nxt = lambda off: (lambda bi, h, qi: (bi, jnp.minimum((qi + 1) * r, nblk - 1), off + h))
    edge = lambda f: pl.BlockSpec((None, WINDOW, HEAD_DIM), f)
    body = lambda f: pl.BlockSpec((None, tq, HEAD_DIM), f)
    return pl.pallas_call(
        functools.partial(_window_kernel, seq=seq),
        grid=(b, A_KV_HEADS, seq // tq),
        in_specs=[
            pl.BlockSpec((None, tq, gw), lambda bi, h, qi: (bi, qi, q_chunks[0] + 2 * h)),
            edge(prev(k0)), body(main(k0)), edge(nxt(k0)),
            edge(prev(v0)), body(main(v0)), edge(nxt(v0)),
            pl.BlockSpec((None, None, 1, GROUP * WINDOW), lambda bi, h, qi: (l, h, 0, 0)),
        ],
        out_specs=pl.BlockSpec((None, tq, gw), lambda bi, h, qi: (bi, qi, h)),
        out_shape=jax.ShapeDtypeStruct((b, seq, A_Q_HEADS * HEAD_DIM), BF16),
        scratch_shapes=[pltpu.VMEM((2, 3 * WINDOW, GROUP * WINDOW), F32)],
        compiler_params=_params(("arbitrary", "arbitrary", "arbitrary")),
        name="window_attn",
    )(z3, z3, z3, z3, z3, z3, z3, sink_rows)


def _dense_kernel(q_ref, k_ref, v_ref, o_ref, vt_scr, s0_scr, s1_scr):
    tq = q_ref.shape[0]
    seq = k_ref.shape[0]
    cols = GROUP * tq

    @pl.when(pl.program_id(2) == 0)
    def _():
        for c in range(seq // DENSE_TK):
            blk = v_ref[c * DENSE_TK:(c + 1) * DENSE_TK, :].astype(F32)
            vt_scr[:HEAD_DIM, c * DENSE_TK:(c + 1) * DENSE_TK] = blk.T.astype(BF16)
        vt_scr[HEAD_DIM:, :] = jnp.ones((DENSE_ONES_ROWS, seq), BF16)

    q = jnp.concatenate([q_ref[:, g * HEAD_DIM:(g + 1) * HEAD_DIM] for g in range(GROUP)], axis=0)

    def scores(ci, s_scr):
        start = pl.multiple_of(ci * DENSE_TK, DENSE_TK)
        k = k_ref[pl.ds(start, DENSE_TK), :]
        s = lax.dot_general(k, q, (((1,), (1,)), ((), ())), preferred_element_type=F32)
        s_scr[...] = s
        return jnp.max(s, axis=0, keepdims=True)

    def update(ci, s_scr, mc, state):
        m, acc = state
        start = pl.multiple_of(ci * DENSE_TK, DENSE_TK)
        vt = vt_scr[:, pl.ds(start, DENSE_TK)]
        m_new = jnp.maximum(m, mc)
        alpha = jnp.exp2(m - m_new)
        p = jnp.exp2(s_scr[...] - m_new).astype(BF16)
        acc = alpha * acc + jnp.dot(vt, p, preferred_element_type=F32)
        return m_new, acc

    n_chunks = seq // DENSE_TK
    bufs = (s0_scr, s1_scr)
    unroll = DENSE_UNROLL_LONG if n_chunks >= DENSE_LONG_CHUNKS else DENSE_UNROLL_SHORT
    unroll = min(unroll, n_chunks)
    assert n_chunks % unroll == 0 and (unroll % 2 == 0 or n_chunks == 1)

    def group(base, mc, state, last):
        for j in range(unroll):
            c = base + j
            nxt = None if (last and j == unroll - 1) else scores(c + 1, bufs[(j + 1) % 2])
            state = update(c, bufs[j % 2], mc, state)
            mc = nxt
        return mc, state

    state = (jnp.full((1, cols), -jnp.inf, F32),
             jnp.zeros((HEAD_DIM + DENSE_ONES_ROWS, cols), F32))
    mc, state = lax.fori_loop(
        0, n_chunks // unroll - 1,
        lambda i, carry: group(i * unroll, carry[0], carry[1], False),
        (scores(0, bufs[0]), state))
    _, (_, acc) = group(n_chunks - unroll, mc, state, True)
    o = acc[:HEAD_DIM] / acc[HEAD_DIM:HEAD_DIM + 1]
    for g in range(GROUP):
        o_ref[:, g * HEAD_DIM:(g + 1) * HEAD_DIM] = o[:, g * tq:(g + 1) * tq].T.astype(BF16)


def _dense_attention(z3):
    b, seq, _ = z3.shape
    tq = DENSE_TQ
    gw = CHUNK_W
    heads_per_chunk = CHUNK_W // HEAD_DIM
    k0 = _z_chunk(KB_OFF) * heads_per_chunk
    v0 = k0 + B_KV_HEADS
    q_chunks = [_z_chunk(QB_OFF + g * CHUNK_W) for g in range(B_KV_HEADS)]
    assert q_chunks[1] - q_chunks[0] == 2
    return pl.pallas_call(
        _dense_kernel,
        grid=(b, B_KV_HEADS, seq // tq),
        in_specs=[
            pl.BlockSpec((None, tq, gw), lambda bi, h, qi: (bi, qi, q_chunks[0] + 2 * h)),
            pl.BlockSpec((None, seq, HEAD_DIM), lambda bi, h, qi: (bi, 0, k0 + h)),
            pl.BlockSpec((None, seq, HEAD_DIM), lambda bi, h, qi: (bi, 0, v0 + h)),
        ],
        out_specs=pl.BlockSpec((None, tq, gw), lambda bi, h, qi: (bi, qi, h)),
        out_shape=jax.ShapeDtypeStruct((b, seq, B_Q_HEADS * HEAD_DIM), BF16),
        scratch_shapes=[pltpu.VMEM((HEAD_DIM + DENSE_ONES_ROWS, seq), BF16),
                        pltpu.VMEM((DENSE_TK, GROUP * tq), F32),
                        pltpu.VMEM((DENSE_TK, GROUP * tq), F32)],
        compiler_params=_params(("arbitrary", "arbitrary", "arbitrary")),
        name="dense_attn",
    )(z3, z3, z3)


def _postmix_kernel(x_ref, oa_ref, ob_ref, *rest):
    n_gate = D_MODEL // CHUNK_W
    ga_refs, gb_refs = rest[:n_gate], rest[n_gate:2 * n_gate]
    wa_ref, wb_ref, wo_ref, g_ref, gate_ref, o_ref, m_scr = rest[2 * n_gate:]
    for r0 in range(0, x_ref.shape[0], POST_ROWS):
        rows = slice(r0, r0 + POST_ROWS)
        oa = oa_ref[rows, :]
        ob = ob_ref[rows, :]
        for ci, (ga_ref, gb_ref) in enumerate(zip(ga_refs, gb_refs)):
            cs = slice(ci * CHUNK_W, (ci + 1) * CHUNK_W)
            pa = jnp.dot(oa, wa_ref[:, cs], preferred_element_type=F32)
            pb = jnp.dot(ob, wb_ref[:, cs], preferred_element_type=F32)
            merged = ga_ref[rows, :].astype(F32) * pa + gb_ref[rows, :].astype(F32) * pb
            m_scr[rows, cs] = merged.astype(BF16)
        y = jnp.dot(m_scr[rows, :], wo_ref[...], preferred_element_type=F32)
        o_ref[rows, :] = x_ref[rows, :] + gate_ref[...] * _rms(y, g_ref[...])


def _postmix(x2d, oa2d, ob2d, z2d, wa, wb, wo, g, mod5, row0, l, seq):
    t, d = x2d.shape
    tm = POST_TM
    nsb = seq // tm
    n_gate = d // CHUNK_W
    row = lambda i: (i, 0)
    resident = lambda shape: pl.BlockSpec((None,) + tuple(shape[1:]), lambda i: (l, 0, 0),
                                          pipeline_mode=pl.Buffered(1))
    gate_blk = lambda off: pl.BlockSpec((tm, CHUNK_W), functools.partial(
        lambda chunk, i: (i, chunk), _z_chunk(off)))
    gate_specs = ([gate_blk(GA_OFF + c * CHUNK_W) for c in range(n_gate)]
                  + [gate_blk(GB_OFF + c * CHUNK_W) for c in range(n_gate)])
    return pl.pallas_call(
        _postmix_kernel,
        grid=(t // tm,),
        in_specs=[
            pl.BlockSpec((tm, d), row),
            pl.BlockSpec((tm, oa2d.shape[1]), row),
            pl.BlockSpec((tm, ob2d.shape[1]), row),
            *gate_specs,
            resident(wa.shape), resident(wb.shape), resident(wo.shape),
            _layer_vec_spec(l, d),
            _mod_spec(MOD_GATE_M, l, row0, nsb),
        ],
        out_specs=pl.BlockSpec((tm, d), row),
        out_shape=jax.ShapeDtypeStruct((t, d), F32),
        scratch_shapes=[pltpu.VMEM((tm, d), BF16)],
        compiler_params=_params(("arbitrary",)),
        name="postmix",
    )(x2d, oa2d, ob2d, *([z2d] * (2 * n_gate)), wa, wb, wo, g, mod5)


def _ffn_kernel(x_ref, g_ref, sc_ref, sh_ref, w1_ref, w3_ref, w2_ref, gpost_ref, gate_ref, o_ref,
                u_scr):
    k = pl.program_id(1)
    last = pl.num_programs(1) - 1

    def chunk(first, rows=slice(None)):
        u = u_scr[rows, :]
        h1 = jnp.dot(u, w1_ref[...], preferred_element_type=F32)
        h3 = jnp.dot(u, w3_ref[...], preferred_element_type=F32)
        h = (h1 * _sigmoid(h1) * h3).astype(BF16)
        part = jnp.dot(h, w2_ref[...], preferred_element_type=F32)
        if first:
            o_ref[rows, :] = part
        else:
            o_ref[rows, :] += part

    @pl.when(k == 0)
    def _():
        u_scr[...] = _mod_norm(x_ref[...], g_ref[...], sc_ref[...], sh_ref[...]).astype(BF16)
        chunk(True)

    pl.when(jnp.logical_and(k > 0, k < last))(functools.partial(chunk, False))

    @pl.when(k == last)
    def _():
        for r0 in range(0, o_ref.shape[0], FFN_TAIL_ROWS):
            rows = slice(r0, r0 + FFN_TAIL_ROWS)
            chunk(False, rows)
            o_ref[rows, :] = x_ref[rows, :] + gate_ref[...] * _rms(o_ref[rows, :], gpost_ref[...])


def _ffn(x2d, g_pre, mod5, row0, l, w13, w2, g_post, seq):
    t, d = x2d.shape
    tm, tf = FFN_TM, FFN_TF
    d_ff = w2.shape[1]
    nk = d_ff // tf
    nsb = seq // tm
    row = lambda i, k: (i, 0)
    return pl.pallas_call(
        _ffn_kernel,
        grid=(t // tm, nk),
        in_specs=[
            pl.BlockSpec((tm, d), row),
            _layer_vec_spec(l, d),
            _mod_spec(MOD_SCALE_F, l, row0, nsb),
            _mod_spec(MOD_SHIFT_F, l, row0, nsb),
            pl.BlockSpec((None, d, tf), lambda i, k: (l, 0, k)),
            pl.BlockSpec((None, d, tf), lambda i, k: (l, 0, nk + k)),
            pl.BlockSpec((None, tf, d), lambda i, k: (l, k, 0)),
            _layer_vec_spec(l, d),
            _mod_spec(MOD_GATE_F, l, row0, nsb),
        ],
        out_specs=pl.BlockSpec((tm, d), row),
        out_shape=jax.ShapeDtypeStruct((t, d), F32),
        scratch_shapes=[pltpu.VMEM((tm, d), BF16)],
        compiler_params=_params(("arbitrary", "arbitrary")),
        name="ffn",
    )(x2d, g_pre, mod5, mod5, w13, w13, w2, g_post, mod5)


def _rope_tables(seq):
    def angles(pos, dim):
        inv_freq = ROPE_THETA ** (-jnp.arange(0, dim, 2, dtype=F32) / dim)
        ang = pos.astype(F32)[:, None] * inv_freq[None, :]
        return jnp.cos(ang), jnp.sin(ang)

    t = jnp.arange(seq)
    c1, s1 = angles(t, HEAD_DIM)
    cr, sr = angles(t // GRID_W, HEAD_DIM // 2)
    cc, sc = angles(t % GRID_W, HEAD_DIM // 2)
    zero = jnp.zeros_like(sr)
    cos1 = jnp.concatenate([c1, c1], axis=-1)
    sin1 = jnp.concatenate([-s1, s1], axis=-1)
    cosx = jnp.concatenate([cr, cr, cc, cc], axis=-1)
    sin_lo = jnp.concatenate([-sr, zero, -sc, zero], axis=-1)
    sin_hi = jnp.concatenate([zero, sr, zero, sc], axis=-1)
    return cos1, sin1, cosx, sin_lo, sin_hi


def _encoder(x, mod5, row0, w):
    b, seq, d = x.shape
    tables = _rope_tables(seq)
    x2d = x.reshape(b * seq, d)
    for l in range(w["w_in"].shape[0]):
        z2d = _inproj(x2d, mod5, row0, l, w["g_pre_mix"], w["w_in"], tables, w["q_norm_b"],
                      w["k_norm_b"], seq)
        z3 = z2d.reshape(b, seq, IN_W)
        oa = _window_attention(z3, w["sink_rows"], l)
        ob = _dense_attention(z3)
        x2d = _postmix(x2d, oa.reshape(b * seq, -1), ob.reshape(b * seq, -1), z2d, w["w_branch_a"],
                       w["w_branch_b"], w["w_out"], w["g_post_mix"], mod5, row0, l, seq)
        x2d = _ffn(x2d, w["g_pre_ffn"], mod5, row0, l, w["w_13"], w["w_2"], w["g_post_ffn"], seq)
    return x2d.reshape(b, seq, d)


def kernel(x_prompt, x_sample, c_prompt, c_sample, g_pre_mix, g_post_mix, g_pre_ffn, g_post_ffn, w_mod,
           b_mod, w_in, q_norm_b, k_norm_b, sink_a, w_branch_a, w_branch_b, w_out, w_13, w_2):
    depth = w_in.shape[0]
    nb_p, nb_s = c_prompt.shape[0], c_sample.shape[0]
    assert nb_p + nb_s <= MOD_ROWS
    c_all = jnp.zeros((MOD_ROWS, D_MODEL), F32)
    c_all = c_all.at[:nb_p].set(c_prompt).at[nb_p:nb_p + nb_s].set(c_sample)
    mod = _modulation(c_all, w_mod, b_mod)
    mod5 = mod.reshape(depth, MOD_ROWS, N_MOD, 1, D_MODEL)

    vec = lambda a: a.reshape(depth, 1, a.shape[-1])
    weights = {
        "g_pre_mix": vec(g_pre_mix), "g_post_mix": vec(g_post_mix),
        "g_pre_ffn": vec(g_pre_ffn), "g_post_ffn": vec(g_post_ffn),
        "q_norm_b": vec(q_norm_b), "k_norm_b": vec(k_norm_b),
        "sink_rows": jnp.broadcast_to(
            sink_a.astype(F32).reshape(depth, A_KV_HEADS, 1, GROUP, 1),
            (depth, A_KV_HEADS, 1, GROUP, WINDOW)).reshape(depth, A_KV_HEADS, 1, GROUP * WINDOW),
        "w_in": jnp.concatenate([w_in[:, :, off:off + CHUNK_W] for off, _ in Z_CHUNKS],
                                axis=2).astype(BF16),
        "w_branch_a": w_branch_a.astype(BF16), "w_branch_b": w_branch_b.astype(BF16),
        "w_out": w_out.astype(BF16),
        "w_13": w_13.astype(BF16), "w_2": w_2.astype(BF16),
    }

    y_prompt = _encoder(x_prompt, mod5, 0, weights)
    y_sample = _encoder(x_sample, mod5, nb_p, weights)
    return (y_prompt, y_sample)
```

```python
import functools

import jax
import jax.numpy as jnp
from jax import lax
from jax.experimental import pallas as pl
from jax.experimental.pallas import tpu as pltpu

D_MODEL = 2048
HEAD_DIM = 128
A_Q_HEADS = 8
A_KV_HEADS = 2
B_Q_HEADS = 8
B_KV_HEADS = 2
GROUP = A_Q_HEADS // A_KV_HEADS
WINDOW = 128
GRID_W = 64
ROPE_THETA = 10000.0
N_MOD = 6
EPS = 1e-6
MASK_VALUE = -1e30
LOG2_E = 1.4426950408889634

QA_OFF = 0
KA_OFF = QA_OFF + A_Q_HEADS * HEAD_DIM
VA_OFF = KA_OFF + A_KV_HEADS * HEAD_DIM
QB_OFF = VA_OFF + A_KV_HEADS * HEAD_DIM
KB_OFF = QB_OFF + B_Q_HEADS * HEAD_DIM
VB_OFF = KB_OFF + B_KV_HEADS * HEAD_DIM
GA_OFF = VB_OFF + B_KV_HEADS * HEAD_DIM
GB_OFF = GA_OFF + D_MODEL
IN_W = GB_OFF + D_MODEL

CHUNK_W = GROUP * HEAD_DIM
Z_CHUNKS = (
    (QA_OFF, "qa"), (GA_OFF, "gate"),
    (QA_OFF + CHUNK_W, "qa"), (GA_OFF + CHUNK_W, "gate"),
    (KA_OFF, "kva"), (GA_OFF + 2 * CHUNK_W, "gate"),
    (QB_OFF, "qb"), (GA_OFF + 3 * CHUNK_W, "gate"),
    (QB_OFF + CHUNK_W, "qb"), (GB_OFF, "gate"),
    (KB_OFF, "kvb"), (GB_OFF + CHUNK_W, "gate"),
    (GB_OFF + 2 * CHUNK_W, "gate"), (GB_OFF + 3 * CHUNK_W, "gate"),
)
assert len(Z_CHUNKS) * CHUNK_W == IN_W


def _z_chunk(src_off):
    return [off for off, _ in Z_CHUNKS].index(src_off)

V7X_VMEM_BYTES = 64 * 1024 * 1024
VMEM_LIMIT = V7X_VMEM_BYTES - 8 * 1024 * 1024

F32 = jnp.float32
BF16 = jnp.bfloat16

MOD_ROWS = 16
MOD_TN = 1024
INPROJ_TM = 1024
INPROJ_TN = 1024
INPROJ_ROWS = 512
WIN_TQ = 1024
DENSE_TQ = 512
DENSE_TK = 512
DENSE_ONES_ROWS = 16
DENSE_UNROLL_SHORT = 2
DENSE_UNROLL_LONG = 4
DENSE_LONG_CHUNKS = 16
POST_TM = 512
POST_ROWS = 256
FFN_TM = 1024
FFN_TF = 256
FFN_TAIL_ROWS = 256


def _params(sem):
    return pltpu.CompilerParams(dimension_semantics=sem, vmem_limit_bytes=VMEM_LIMIT)


def _mod_kernel(c_ref, w_ref, b_ref, o_ref):
    c = c_ref[...]
    a = (c * jax.nn.sigmoid(c)).astype(BF16)
    w = w_ref[...].astype(BF16)
    o_ref[...] = jnp.dot(a, w, preferred_element_type=F32) + b_ref[...]


def _modulation(c_all, w_mod, b_mod):
    depth, d, n = w_mod.shape
    return pl.pallas_call(
        _mod_kernel,
        grid=(depth, n // MOD_TN),
        in_specs=[
            pl.BlockSpec((MOD_ROWS, d), lambda l, j: (0, 0)),
            pl.BlockSpec((None, d, MOD_TN), lambda l, j: (l, 0, j)),
            pl.BlockSpec((None, 1, MOD_TN), lambda l, j: (l, 0, j)),
        ],
        out_specs=pl.BlockSpec((None, MOD_ROWS, MOD_TN), lambda l, j: (l, 0, j)),
        out_shape=jax.ShapeDtypeStruct((depth, MOD_ROWS, n), F32),
        compiler_params=_params(("arbitrary", "arbitrary")),
        name="modulation",
    )(c_all, w_mod, b_mod.reshape(depth, 1, n))


def _mod_norm(x, g, scale, shift):
    ms = jnp.mean(x * x, axis=-1, keepdims=True)
    y = x * lax.rsqrt(ms + EPS) * g
    return y * (1.0 + scale) + shift


def _sigmoid(x):
    return 0.5 * jnp.tanh(0.5 * x) + 0.5


def _rms(y, g):
    ms = jnp.mean(y * y, axis=-1, keepdims=True)
    return y * lax.rsqrt(ms + EPS) * g


def _rope1d(h, cos, sin_signed):
    return h * cos + pltpu.roll(h, HEAD_DIM // 2, 1) * sin_signed


def _rope_axial(h, cos, sin_lo, sin_hi):
    q = HEAD_DIM // 4
    return h * cos + pltpu.roll(h, HEAD_DIM - q, 1) * sin_lo + pltpu.roll(h, q, 1) * sin_hi


def _inproj_kernel(x_ref, g_ref, sc_ref, sh_ref, w_ref, cos1_ref, sin1_ref, cosx_ref, sinlo_ref,
                   sinhi_ref, qn_ref, kn_ref, z_ref, u_scr):
    j = pl.program_id(1)
    qscale = HEAD_DIM ** -0.5 * LOG2_E

    def rope_a(v, rows, scale):
        r = _rope1d(v, cos1_ref[rows, :], sin1_ref[rows, :])
        return r * scale if scale != 1.0 else r

    def rope_b(v, rows, gain):
        ms = jnp.mean(v * v, axis=-1, keepdims=True)
        vn = v * lax.rsqrt(ms + EPS) * gain
        return _rope_axial(vn, cosx_ref[rows, :], sinlo_ref[rows, :], sinhi_ref[rows, :])

    q_a = lambda v, rows: rope_a(v, rows, qscale)
    k_a = lambda v, rows: rope_a(v, rows, 1.0)
    q_b = lambda v, rows: rope_b(v, rows, qn_ref[...] * qscale)
    k_b = lambda v, rows: rope_b(v, rows, kn_ref[...])
    keep = lambda v, rows: v
    head_ops = {
        "qa": [q_a] * GROUP,
        "kva": [k_a] * A_KV_HEADS + [keep] * A_KV_HEADS,
        "qb": [q_b] * GROUP,
        "kvb": [k_b] * B_KV_HEADS + [keep] * B_KV_HEADS,
    }
    chunks_per_block = INPROJ_TN // CHUNK_W
    tm = u_scr.shape[0]

    def run(kinds):
        for ci, kind in enumerate(kinds):
            cols = slice(ci * CHUNK_W, (ci + 1) * CHUNK_W)
            for r0 in range(0, tm, INPROJ_ROWS):
                rows = slice(r0, r0 + INPROJ_ROWS)
                acc = jnp.dot(u_scr[rows, :], w_ref[:, cols], preferred_element_type=F32)
                if kind == "gate":
                    z_ref[rows, cols] = _sigmoid(acc).astype(BF16)
                    continue
                for hh, op in enumerate(head_ops[kind]):
                    lo = ci * CHUNK_W + hh * HEAD_DIM
                    v = op(acc[:, hh * HEAD_DIM:(hh + 1) * HEAD_DIM], rows)
                    z_ref[rows, lo:lo + HEAD_DIM] = v.astype(BF16)

    n_blocks = len(Z_CHUNKS) // chunks_per_block
    block_kinds = [tuple(k for _, k in Z_CHUNKS[bj * chunks_per_block:(bj + 1) * chunks_per_block])
                   for bj in range(n_blocks)]
    @pl.when(j == 0)
    def _():
        u_scr[...] = _mod_norm(x_ref[...], g_ref[...], sc_ref[...], sh_ref[...]).astype(BF16)
        run(block_kinds[0])

    for kinds in dict.fromkeys(block_kinds[1:]):
        cond = functools.reduce(jnp.logical_or, [j == bj for bj in range(1, n_blocks)
                                                 if block_kinds[bj] == kinds])
        pl.when(cond)(functools.partial(run, kinds))


MOD_SHIFT_M, MOD_SCALE_M, MOD_GATE_M, MOD_SHIFT_F, MOD_SCALE_F, MOD_GATE_F = range(N_MOD)


def _mod_spec(which, l, row0, rows_per_batch_block):
    return pl.BlockSpec(
        (None, None, None, 1, D_MODEL),
        lambda i, *_: (l, row0 + i // rows_per_batch_block, which, 0, 0))


def _layer_vec_spec(l, width):
    return pl.BlockSpec((None, 1, width), lambda *_: (l, 0, 0))


def _inproj(x2d, mod5, row0, l, g, w_in, tables, qn, kn, seq):
    t, d = x2d.shape
    tm, tn = INPROJ_TM, INPROJ_TN
    nsb = seq // tm
    row = lambda i, j: (i, 0)
    pos = lambda i, j: (i % nsb, 0)
    tab_spec = pl.BlockSpec((tm, HEAD_DIM), pos)
    return pl.pallas_call(
        _inproj_kernel,
        grid=(t // tm, IN_W // tn),
        in_specs=[
            pl.BlockSpec((tm, d), row),
            _layer_vec_spec(l, d),
            _mod_spec(MOD_SCALE_M, l, row0, nsb),
            _mod_spec(MOD_SHIFT_M, l, row0, nsb),
            pl.BlockSpec((None, d, tn), lambda i, j: (l, 0, j)),
            tab_spec, tab_spec, tab_spec, tab_spec, tab_spec,
            _layer_vec_spec(l, HEAD_DIM),
            _layer_vec_spec(l, HEAD_DIM),
        ],
        out_specs=pl.BlockSpec((tm, tn), lambda i, j: (i, j)),
        out_shape=jax.ShapeDtypeStruct((t, IN_W), BF16),
        scratch_shapes=[pltpu.VMEM((tm, d), BF16)],
        compiler_params=_params(("arbitrary", "arbitrary")),
        name="inproj",
    )(x2d, g, mod5, mod5, w_in, *tables, qn, kn)


def _window_kernel(q_ref, kp_ref, km_ref, kn_ref, vp_ref, vm_ref, vn_ref, sink_ref, o_ref, s_scr, *,
                   seq):
    qi = pl.program_id(2)
    tq = q_ref.shape[0]
    kcat = jnp.concatenate([kp_ref[...], km_ref[...], kn_ref[...]], axis=0)
    vcat = jnp.concatenate([vp_ref[...], vm_ref[...], vn_ref[...]], axis=0)
    vt = jnp.concatenate([vcat.astype(F32).T.astype(BF16),
                          jnp.ones((DENSE_ONES_ROWS, tq + 2 * WINDOW), BF16)], axis=0)
    sink = sink_ref[...] * LOG2_E
    lanes = GROUP * WINDOW
    r = lax.broadcasted_iota(jnp.int32, (1, lanes), 1) & (WINDOW - 1)
    kk = lax.broadcasted_iota(jnp.int32, (WINDOW, lanes), 0)
    n_sub = tq // WINDOW

    def masked_scores(sb):
        base = qi * tq + (sb - 1) * WINDOW
        lo = jnp.maximum(r, -base)
        hi = jnp.minimum(r + 2 * WINDOW, seq - 1 - base)
        qb = q_ref[sb * WINDOW:(sb + 1) * WINDOW, :]
        qs = jnp.concatenate([qb[:, g * HEAD_DIM:(g + 1) * HEAD_DIM] for g in range(GROUP)], axis=0)
        kj = kcat[sb * WINDOW:(sb + 3) * WINDOW]
        s = lax.dot_general(kj, qs, (((1,), (1,)), ((), ())), preferred_element_type=F32)
        top = jnp.where(kk >= lo, s[:WINDOW], MASK_VALUE)
        mid = s[WINDOW:2 * WINDOW]
        bot = jnp.where(kk + 2 * WINDOW <= hi, s[2 * WINDOW:], MASK_VALUE)
        s_scr[sb % 2, 0:WINDOW] = top
        s_scr[sb % 2, WINDOW:2 * WINDOW] = mid
        s_scr[sb % 2, 2 * WINDOW:] = bot
        col_max = jnp.maximum(jnp.maximum(jnp.max(top, axis=0, keepdims=True),
                                          jnp.max(mid, axis=0, keepdims=True)),
                              jnp.max(bot, axis=0, keepdims=True))
        return jnp.maximum(col_max, sink)

    def finish(sb, m):
        p = jnp.exp2(s_scr[sb % 2] - m).astype(BF16)
        acc = jnp.dot(vt[:, sb * WINDOW:(sb + 3) * WINDOW], p, preferred_element_type=F32)
        denom = acc[HEAD_DIM:HEAD_DIM + 1] + jnp.exp2(sink - m)
        ot = acc[:HEAD_DIM] / denom
        for g in range(GROUP):
            o_ref[sb * WINDOW:(sb + 1) * WINDOW, g * HEAD_DIM:(g + 1) * HEAD_DIM] = (
                ot[:, g * WINDOW:(g + 1) * WINDOW].T.astype(BF16))

    m = masked_scores(0)
    for sb in range(n_sub):
        m_next = masked_scores(sb + 1) if sb + 1 < n_sub else None
        finish(sb, m)
        m = m_next


def _window_attention(z3, sink_rows, l):
    b, seq, _ = z3.shape
    tq = WIN_TQ
    r = tq // WINDOW
    nblk = seq // WINDOW
    gw = CHUNK_W
    heads_per_chunk = CHUNK_W // HEAD_DIM
    k0 = _z_chunk(KA_OFF) * heads_per_chunk
    v0 = k0 + A_KV_HEADS
    q_chunks = [_z_chunk(QA_OFF + g * CHUNK_W) for g in range(A_KV_HEADS)]
    assert q_chunks[1] - q_chunks[0] == 2
    prev = lambda off: (lambda bi, h, qi: (bi, jnp.maximum(qi * r - 1, 0), off + h))
    main = lambda off: (lambda bi, h, qi: (bi, qi, off + h))
    nxt = lambda off: (lambda bi, h, qi: (bi, jnp.minimum((qi + 1) * r, nblk - 1), off + h))
    edge = lambda f: pl.BlockSpec((None, WINDOW, HEAD_DIM), f)
    body = lambda f: pl.BlockSpec((None, tq, HEAD_DIM), f)
    return pl.pallas_call(
        functools.partial(_window_kernel, seq=seq),
        grid=(b, A_KV_HEADS, seq // tq),
        in_specs=[
            pl.BlockSpec((None, tq, gw), lambda bi, h, qi: (bi, qi, q_chunks[0] + 2 * h)),
            edge(prev(k0)), body(main(k0)), edge(nxt(k0)),
            edge(prev(v0)), body(main(v0)), edge(nxt(v0)),
            pl.BlockSpec((None, None, 1, GROUP * WINDOW), lambda bi, h, qi: (l, h, 0, 0)),
        ],
        out_specs=pl.BlockSpec((None, tq, gw), lambda bi, h, qi: (bi, qi, h)),
        out_shape=jax.ShapeDtypeStruct((b, seq, A_Q_HEADS * HEAD_DIM), BF16),
        scratch_shapes=[pltpu.VMEM((2, 3 * WINDOW, GROUP * WINDOW), F32)],
        compiler_params=_params(("arbitrary", "arbitrary", "arbitrary")),
        name="window_attn",
    )(z3, z3, z3, z3, z3, z3, z3, sink_rows)


def _dense_kernel(q_ref, k_ref, v_ref, o_ref, vt_scr, s0_scr, s1_scr):
    tq = q_ref.shape[0]
    seq = k_ref.shape[0]
    cols = GROUP * tq

    @pl.when(pl.program_id(2) == 0)
    def _():
        for c in range(seq // DENSE_TK):
            blk = v_ref[c * DENSE_TK:(c + 1) * DENSE_TK, :].astype(F32)
            vt_scr[:HEAD_DIM, c * DENSE_TK:(c + 1) * DENSE_TK] = blk.T.astype(BF16)
        vt_scr[HEAD_DIM:, :] = jnp.ones((DENSE_ONES_ROWS, seq), BF16)

    q = jnp.concatenate([q_ref[:, g * HEAD_DIM:(g + 1) * HEAD_DIM] for g in range(GROUP)], axis=0)

    def scores(ci, s_scr):
        start = pl.multiple_of(ci * DENSE_TK, DENSE_TK)
        k = k_ref[pl.ds(start, DENSE_TK), :]
        s = lax.dot_general(k, q, (((1,), (1,)), ((), ())), preferred_element_type=F32)
        s_scr[...] = s
        return jnp.max(s, axis=0, keepdims=True)

    def update(ci, s_scr, mc, state):
        m, acc = state
        start = pl.multiple_of(ci * DENSE_TK, DENSE_TK)
        vt = vt_scr[:, pl.ds(start, DENSE_TK)]
        m_new = jnp.maximum(m, mc)
        alpha = jnp.exp2(m - m_new)
        p = jnp.exp2(s_scr[...] - m_new).astype(BF16)
        acc = alpha * acc + jnp.dot(vt, p, preferred_element_type=F32)
        return m_new, acc

    n_chunks = seq // DENSE_TK
    bufs = (s0_scr, s1_scr)
    unroll = DENSE_UNROLL_LONG if n_chunks >= DENSE_LONG_CHUNKS else DENSE_UNROLL_SHORT
    unroll = min(unroll, n_chunks)
    assert n_chunks % unroll == 0 and (unroll % 2 == 0 or n_chunks == 1)

    def group(base, mc, state, last):
        for j in range(unroll):
            c = base + j
            nxt = None if (last and j == unroll - 1) else scores(c + 1, bufs[(j + 1) % 2])
            state = update(c, bufs[j % 2], mc, state)
            mc = nxt
        return mc, state

    state = (jnp.full((1, cols), -jnp.inf, F32),
             jnp.zeros((HEAD_DIM + DENSE_ONES_ROWS, cols), F32))
    mc, state = lax.fori_loop(
        0, n_chunks // unroll - 1,
        lambda i, carry: group(i * unroll, carry[0], carry[1], False),
        (scores(0, bufs[0]), state))
    _, (_, acc) = group(n_chunks - unroll, mc, state, True)
    o = acc[:HEAD_DIM] / acc[HEAD_DIM:HEAD_DIM + 1]
    for g in range(GROUP):
        o_ref[:, g * HEAD_DIM:(g + 1) * HEAD_DIM] = o[:, g * tq:(g + 1) * tq].T.astype(BF16)


def _dense_attention(z3):
    b, seq, _ = z3.shape
    tq = min(DENSE_TQ, seq)
    gw = CHUNK_W
    heads_per_chunk = CHUNK_W // HEAD_DIM
    k0 = _z_chunk(KB_OFF) * heads_per_chunk
    v0 = k0 + B_KV_HEADS
    q_chunks = [_z_chunk(QB_OFF + g * CHUNK_W) for g in range(B_KV_HEADS)]
    assert q_chunks[1] - q_chunks[0] == 2
    return pl.pallas_call(
        _dense_kernel,
        grid=(b, B_KV_HEADS, seq // tq),
        in_specs=[
            pl.BlockSpec((None, tq, gw), lambda bi, h, qi: (bi, qi, q_chunks[0] + 2 * h)),
            pl.BlockSpec((None, seq, HEAD_DIM), lambda bi, h, qi: (bi, 0, k0 + h)),
            pl.BlockSpec((None, seq, HEAD_DIM), lambda bi, h, qi: (bi, 0, v0 + h)),
        ],
        out_specs=pl.BlockSpec((None, tq, gw), lambda bi, h, qi: (bi, qi, h)),
        out_shape=jax.ShapeDtypeStruct((b, seq, B_Q_HEADS * HEAD_DIM), BF16),
        scratch_shapes=[pltpu.VMEM((HEAD_DIM + DENSE_ONES_ROWS, seq), BF16),
                        pltpu.VMEM((DENSE_TK, GROUP * tq), F32),
                        pltpu.VMEM((DENSE_TK, GROUP * tq), F32)],
        compiler_params=_params(("arbitrary", "arbitrary", "arbitrary")),
        name="dense_attn",
    )(z3, z3, z3)


def _postmix_kernel(x_ref, oa_ref, ob_ref, *rest):
    n_gate = D_MODEL // CHUNK_W
    ga_refs, gb_refs = rest[:n_gate], rest[n_gate:2 * n_gate]
    wa_ref, wb_ref, wo_ref, g_ref, gate_ref, o_ref, m_scr = rest[2 * n_gate:]
    for r0 in range(0, x_ref.shape[0], POST_ROWS):
        rows = slice(r0, r0 + POST_ROWS)
        oa = oa_ref[rows, :]
        ob = ob_ref[rows, :]
        for ci, (ga_ref, gb_ref) in enumerate(zip(ga_refs, gb_refs)):
            cs = slice(ci * CHUNK_W, (ci + 1) * CHUNK_W)
            pa = jnp.dot(oa, wa_ref[:, cs], preferred_element_type=F32)
            pb = jnp.dot(ob, wb_ref[:, cs], preferred_element_type=F32)
            merged = ga_ref[rows, :].astype(F32) * pa + gb_ref[rows, :].astype(F32) * pb
            m_scr[rows, cs] = merged.astype(BF16)
        y = jnp.dot(m_scr[rows, :], wo_ref[...], preferred_element_type=F32)
        o_ref[rows, :] = x_ref[rows, :] + gate_ref[...] * _rms(y, g_ref[...])


def _postmix(x2d, oa2d, ob2d, z2d, wa, wb, wo, g, mod5, row0, l, seq):
    t, d = x2d.shape
    tm = POST_TM
    nsb = seq // tm
    n_gate = d // CHUNK_W
    row = lambda i: (i, 0)
    resident = lambda shape: pl.BlockSpec((None,) + tuple(shape[1:]), lambda i: (l, 0, 0),
                                          pipeline_mode=pl.Buffered(1))
    gate_blk = lambda off: pl.BlockSpec((tm, CHUNK_W), functools.partial(
        lambda chunk, i: (i, chunk), _z_chunk(off)))
    gate_specs = ([gate_blk(GA_OFF + c * CHUNK_W) for c in range(n_gate)]
                  + [gate_blk(GB_OFF + c * CHUNK_W) for c in range(n_gate)])
    return pl.pallas_call(
        _postmix_kernel,
        grid=(t // tm,),
        in_specs=[
            pl.BlockSpec((tm, d), row),
            pl.BlockSpec((tm, oa2d.shape[1]), row),
            pl.BlockSpec((tm, ob2d.shape[1]), row),
            *gate_specs,
            resident(wa.shape), resident(wb.shape), resident(wo.shape),
            _layer_vec_spec(l, d),
            _mod_spec(MOD_GATE_M, l, row0, nsb),
        ],
        out_specs=pl.BlockSpec((tm, d), row),
        out_shape=jax.ShapeDtypeStruct((t, d), F32),
        scratch_shapes=[pltpu.VMEM((tm, d), BF16)],
        compiler_params=_params(("arbitrary",)),
        name="postmix",
    )(x2d, oa2d, ob2d, *([z2d] * (2 * n_gate)), wa, wb, wo, g, mod5)


def _ffn_kernel(x_ref, g_ref, sc_ref, sh_ref, w1_ref, w3_ref, w2_ref, gpost_ref, gate_ref, o_ref,
                u_scr):
    k = pl.program_id(1)
    last = pl.num_programs(1) - 1

    def chunk(first, rows=slice(None)):
        u = u_scr[rows, :]
        h1 = jnp.dot(u, w1_ref[...], preferred_element_type=F32)
        h3 = jnp.dot(u, w3_ref[...], preferred_element_type=F32)
        h = (h1 * _sigmoid(h1) * h3).astype(BF16)
        part = jnp.dot(h, w2_ref[...], preferred_element_type=F32)
        if first:
            o_ref[rows, :] = part
        else:
            o_ref[rows, :] += part

    @pl.when(k == 0)
    def _():
        u_scr[...] = _mod_norm(x_ref[...], g_ref[...], sc_ref[...], sh_ref[...]).astype(BF16)
        chunk(True)

    pl.when(jnp.logical_and(k > 0, k < last))(functools.partial(chunk, False))

    @pl.when(k == last)
    def _():
        for r0 in range(0, o_ref.shape[0], FFN_TAIL_ROWS):
            rows = slice(r0, r0 + FFN_TAIL_ROWS)
            chunk(False, rows)
            o_ref[rows, :] = x_ref[rows, :] + gate_ref[...] * _rms(o_ref[rows, :], gpost_ref[...])


def _ffn(x2d, g_pre, mod5, row0, l, w13, w2, g_post, seq):
    t, d = x2d.shape
    tm, tf = FFN_TM, FFN_TF
    d_ff = w2.shape[1]
    nk = d_ff // tf
    nsb = seq // tm
    row = lambda i, k: (i, 0)
    return pl.pallas_call(
        _ffn_kernel,
        grid=(t // tm, nk),
        in_specs=[
            pl.BlockSpec((tm, d), row),
            _layer_vec_spec(l, d),
            _mod_spec(MOD_SCALE_F, l, row0, nsb),
            _mod_spec(MOD_SHIFT_F, l, row0, nsb),
            pl.BlockSpec((None, d, tf), lambda i, k: (l, 0, k)),
            pl.BlockSpec((None, d, tf), lambda i, k: (l, 0, nk + k)),
            pl.BlockSpec((None, tf, d), lambda i, k: (l, k, 0)),
            _layer_vec_spec(l, d),
            _mod_spec(MOD_GATE_F, l, row0, nsb),
        ],
        out_specs=pl.BlockSpec((tm, d), row),
        out_shape=jax.ShapeDtypeStruct((t, d), F32),
        scratch_shapes=[pltpu.VMEM((tm, d), BF16)],
        compiler_params=_params(("arbitrary", "arbitrary")),
        name="ffn",
    )(x2d, g_pre, mod5, mod5, w13, w13, w2, g_post, mod5)


def _rope_tables(seq):
    def angles(pos, dim):
        inv_freq = ROPE_THETA ** (-jnp.arange(0, dim, 2, dtype=F32) / dim)
        ang = pos.astype(F32)[:, None] * inv_freq[None, :]
        return jnp.cos(ang), jnp.sin(ang)

    t = jnp.arange(seq)
    c1, s1 = angles(t, HEAD_DIM)
    cr, sr = angles(t // GRID_W, HEAD_DIM // 2)
    cc, sc = angles(t % GRID_W, HEAD_DIM // 2)
    zero = jnp.zeros_like(sr)
    cos1 = jnp.concatenate([c1, c1], axis=-1)
    sin1 = jnp.concatenate([-s1, s1], axis=-1)
    cosx = jnp.concatenate([cr, cr, cc, cc], axis=-1)
    sin_lo = jnp.concatenate([-sr, zero, -sc, zero], axis=-1)
    sin_hi = jnp.concatenate([zero, sr, zero, sc], axis=-1)
    return cos1, sin1, cosx, sin_lo, sin_hi


def _encoder(x, mod5, row0, w):
    b, seq, d = x.shape
    tables = _rope_tables(seq)
    x2d = x.reshape(b * seq, d)
    for l in range(w["w_in"].shape[0]):
        z2d = _inproj(x2d, mod5, row0, l, w["g_pre_mix"], w["w_in"], tables, w["q_norm_b"],
                      w["k_norm_b"], seq)
        z3 = z2d.reshape(b, seq, IN_W)
        oa = _window_attention(z3, w["sink_rows"], l)
        ob = _dense_attention(z3)
        x2d = _postmix(x2d, oa.reshape(b * seq, -1), ob.reshape(b * seq, -1), z2d, w["w_branch_a"],
                       w["w_branch_b"], w["w_out"], w["g_post_mix"], mod5, row0, l, seq)
        x2d = _ffn(x2d, w["g_pre_ffn"], mod5, row0, l, w["w_13"], w["w_2"], w["g_post_ffn"], seq)
    return x2d.reshape(b, seq, d)


def kernel(x_prompt, x_sample, c_prompt, c_sample, g_pre_mix, g_post_mix, g_pre_ffn, g_post_ffn, w_mod,
           b_mod, w_in, q_norm_b, k_norm_b, sink_a, w_branch_a, w_branch_b, w_out, w_13, w_2):
    depth = w_in.shape[0]
    nb_p, nb_s = c_prompt.shape[0], c_sample.shape[0]
    assert nb_p + nb_s <= MOD_ROWS
    c_all = jnp.zeros((MOD_ROWS, D_MODEL), F32)
    c_all = c_all.at[:nb_p].set(c_prompt).at[nb_p:nb_p + nb_s].set(c_sample)
    mod = _modulation(c_all, w_mod, b_mod)
    mod5 = mod.reshape(depth, MOD_ROWS, N_MOD, 1, D_MODEL)

    vec = lambda a: a.reshape(depth, 1, a.shape[-1])
    weights = {
        "g_pre_mix": vec(g_pre_mix), "g_post_mix": vec(g_post_mix),
        "g_pre_ffn": vec(g_pre_ffn), "g_post_ffn": vec(g_post_ffn),
        "q_norm_b": vec(q_norm_b), "k_norm_b": vec(k_norm_b),
        "sink_rows": jnp.broadcast_to(
            sink_a.astype(F32).reshape(depth, A_KV_HEADS, 1, GROUP, 1),
            (depth, A_KV_HEADS, 1, GROUP, WINDOW)).reshape(depth, A_KV_HEADS, 1, GROUP * WINDOW),
        "w_in": jnp.concatenate([w_in[:, :, off:off + CHUNK_W] for off, _ in Z_CHUNKS],
                                axis=2).astype(BF16),
        "w_branch_a": w_branch_a.astype(BF16), "w_branch_b": w_branch_b.astype(BF16),
        "w_out": w_out.astype(BF16),
        "w_13": w_13.astype(BF16), "w_2": w_2.astype(BF16),
    }

    y_prompt = _encoder(x_prompt, mod5, 0, weights)
    y_sample = _encoder(x_sample, mod5, nb_p, weights)
    return (y_prompt, y_sample)
```

```python
import functools

import jax
import jax.numpy as jnp
from jax import lax
from jax.experimental import pallas as pl
from jax.experimental.pallas import tpu as pltpu

D_MODEL = 2048
HEAD_DIM = 128
A_Q_HEADS = 8
A_KV_HEADS = 2
B_Q_HEADS = 8
B_KV_HEADS = 2
GROUP = A_Q_HEADS // A_KV_HEADS
WINDOW = 128
GRID_W = 64
ROPE_THETA = 10000.0
N_MOD = 6
EPS = 1e-6
MASK_VALUE = -1e30
LOG2_E = 1.4426950408889634

QA_OFF = 0
KA_OFF = QA_OFF + A_Q_HEADS * HEAD_DIM
VA_OFF = KA_OFF + A_KV_HEADS * HEAD_DIM
QB_OFF = VA_OFF + A_KV_HEADS * HEAD_DIM
KB_OFF = QB_OFF + B_Q_HEADS * HEAD_DIM
VB_OFF = KB_OFF + B_KV_HEADS * HEAD_DIM
GA_OFF = VB_OFF + B_KV_HEADS * HEAD_DIM
GB_OFF = GA_OFF + D_MODEL
IN_W = GB_OFF + D_MODEL

CHUNK_W = GROUP * HEAD_DIM
Z_CHUNKS = (
    (QA_OFF, "qa"), (GA_OFF, "gate"),
    (QA_OFF + CHUNK_W, "qa"), (GA_OFF + CHUNK_W, "gate"),
    (KA_OFF, "kva"), (GA_OFF + 2 * CHUNK_W, "gate"),
    (QB_OFF, "qb"), (GA_OFF + 3 * CHUNK_W, "gate"),
    (QB_OFF + CHUNK_W, "qb"), (GB_OFF, "gate"),
    (KB_OFF, "kvb"), (GB_OFF + CHUNK_W, "gate"),
    (GB_OFF + 2 * CHUNK_W, "gate"), (GB_OFF + 3 * CHUNK_W, "gate"),
)
assert len(Z_CHUNKS) * CHUNK_W == IN_W


def _z_chunk(src_off):
    return [off for off, _ in Z_CHUNKS].index(src_off)

V7X_VMEM_BYTES = 64 * 1024 * 1024
VMEM_LIMIT = V7X_VMEM_BYTES - 8 * 1024 * 1024

F32 = jnp.float32
BF16 = jnp.bfloat16

MOD_ROWS = 16
MOD_TN = 1024
INPROJ_TM = 1024
INPROJ_TN = 1024
INPROJ_ROWS = 512
WIN_TQ = 1024
DENSE_TQ_SHORT = 512
DENSE_TQ_LONG = 256
DENSE_TK = 512
DENSE_ONES_ROWS = 16
DENSE_UNROLL_SHORT = 2
DENSE_UNROLL_LONG = 4
DENSE_LONG_CHUNKS = 16
POST_TM = 512
POST_ROWS = 256
FFN_TM = 1024
FFN_TF = 256
FFN_TAIL_ROWS = 256


def _params(sem):
    return pltpu.CompilerParams(dimension_semantics=sem, vmem_limit_bytes=VMEM_LIMIT)


def _mod_kernel(c_ref, w_ref, b_ref, o_ref):
    c = c_ref[...]
    a = (c * jax.nn.sigmoid(c)).astype(BF16)
    w = w_ref[...].astype(BF16)
    o_ref[...] = jnp.dot(a, w, preferred_element_type=F32) + b_ref[...]


def _modulation(c_all, w_mod, b_mod):
    depth, d, n = w_mod.shape
    return pl.pallas_call(
        _mod_kernel,
        grid=(depth, n // MOD_TN),
        in_specs=[
            pl.BlockSpec((MOD_ROWS, d), lambda l, j: (0, 0)),
            pl.BlockSpec((None, d, MOD_TN), lambda l, j: (l, 0, j)),
            pl.BlockSpec((None, 1, MOD_TN), lambda l, j: (l, 0, j)),
        ],
        out_specs=pl.BlockSpec((None, MOD_ROWS, MOD_TN), lambda l, j: (l, 0, j)),
        out_shape=jax.ShapeDtypeStruct((depth, MOD_ROWS, n), F32),
        compiler_params=_params(("arbitrary", "arbitrary")),
        name="modulation",
    )(c_all, w_mod, b_mod.reshape(depth, 1, n))


def _mod_norm(x, g, scale, shift):
    ms = jnp.mean(x * x, axis=-1, keepdims=True)
    y = x * lax.rsqrt(ms + EPS) * g
    return y * (1.0 + scale) + shift


def _sigmoid(x):
    return 0.5 * jnp.tanh(0.5 * x) + 0.5


def _rms(y, g):
    ms = jnp.mean(y * y, axis=-1, keepdims=True)
    return y * lax.rsqrt(ms + EPS) * g


def _rope1d(h, cos, sin_signed):
    return h * cos + pltpu.roll(h, HEAD_DIM // 2, 1) * sin_signed


def _rope_axial(h, cos, sin_lo, sin_hi):
    q = HEAD_DIM // 4
    return h * cos + pltpu.roll(h, HEAD_DIM - q, 1) * sin_lo + pltpu.roll(h, q, 1) * sin_hi


def _inproj_kernel(x_ref, g_ref, sc_ref, sh_ref, w_ref, cos1_ref, sin1_ref, cosx_ref, sinlo_ref,
                   sinhi_ref, qn_ref, kn_ref, z_ref, u_scr):
    j = pl.program_id(1)
    qscale = HEAD_DIM ** -0.5 * LOG2_E

    def rope_a(v, rows, scale):
        r = _rope1d(v, cos1_ref[rows, :], sin1_ref[rows, :])
        return r * scale if scale != 1.0 else r

    def rope_b(v, rows, gain):
        ms = jnp.mean(v * v, axis=-1, keepdims=True)
        vn = v * lax.rsqrt(ms + EPS) * gain
        return _rope_axial(vn, cosx_ref[rows, :], sinlo_ref[rows, :], sinhi_ref[rows, :])

    q_a = lambda v, rows: rope_a(v, rows, qscale)
    k_a = lambda v, rows: rope_a(v, rows, 1.0)
    q_b = lambda v, rows: rope_b(v, rows, qn_ref[...] * qscale)
    k_b = lambda v, rows: rope_b(v, rows, kn_ref[...])
    keep = lambda v, rows: v
    head_ops = {
        "qa": [q_a] * GROUP,
        "kva": [k_a] * A_KV_HEADS + [keep] * A_KV_HEADS,
        "qb": [q_b] * GROUP,
        "kvb": [k_b] * B_KV_HEADS + [keep] * B_KV_HEADS,
    }
    chunks_per_block = INPROJ_TN // CHUNK_W
    tm = u_scr.shape[0]

    def run(kinds):
        for ci, kind in enumerate(kinds):
            cols = slice(ci * CHUNK_W, (ci + 1) * CHUNK_W)
            for r0 in range(0, tm, INPROJ_ROWS):
                rows = slice(r0, r0 + INPROJ_ROWS)
                acc = jnp.dot(u_scr[rows, :], w_ref[:, cols], preferred_element_type=F32)
                if kind == "gate":
                    z_ref[rows, cols] = _sigmoid(acc).astype(BF16)
                    continue
                for hh, op in enumerate(head_ops[kind]):
                    lo = ci * CHUNK_W + hh * HEAD_DIM
                    v = op(acc[:, hh * HEAD_DIM:(hh + 1) * HEAD_DIM], rows)
                    z_ref[rows, lo:lo + HEAD_DIM] = v.astype(BF16)

    n_blocks = len(Z_CHUNKS) // chunks_per_block
    block_kinds = [tuple(k for _, k in Z_CHUNKS[bj * chunks_per_block:(bj + 1) * chunks_per_block])
                   for bj in range(n_blocks)]
    @pl.when(j == 0)
    def _():
        u_scr[...] = _mod_norm(x_ref[...], g_ref[...], sc_ref[...], sh_ref[...]).astype(BF16)
        run(block_kinds[0])

    for kinds in dict.fromkeys(block_kinds[1:]):
        cond = functools.reduce(jnp.logical_or, [j == bj for bj in range(1, n_blocks)
                                                 if block_kinds[bj] == kinds])
        pl.when(cond)(functools.partial(run, kinds))


MOD_SHIFT_M, MOD_SCALE_M, MOD_GATE_M, MOD_SHIFT_F, MOD_SCALE_F, MOD_GATE_F = range(N_MOD)


def _mod_spec(which, l, row0, rows_per_batch_block):
    return pl.BlockSpec(
        (None, None, None, 1, D_MODEL),
        lambda i, *_: (l, row0 + i // rows_per_batch_block, which, 0, 0))


def _layer_vec_spec(l, width):
    return pl.BlockSpec((None, 1, width), lambda *_: (l, 0, 0))


def _inproj(x2d, mod5, row0, l, g, w_in, tables, qn, kn, seq):
    t, d = x2d.shape
    tm, tn = INPROJ_TM, INPROJ_TN
    nsb = seq // tm
    row = lambda i, j: (i, 0)
    pos = lambda i, j: (i % nsb, 0)
    tab_spec = pl.BlockSpec((tm, HEAD_DIM), pos)
    return pl.pallas_call(
        _inproj_kernel,
        grid=(t // tm, IN_W // tn),
        in_specs=[
            pl.BlockSpec((tm, d), row),
            _layer_vec_spec(l, d),
            _mod_spec(MOD_SCALE_M, l, row0, nsb),
            _mod_spec(MOD_SHIFT_M, l, row0, nsb),
            pl.BlockSpec((None, d, tn), lambda i, j: (l, 0, j)),
            tab_spec, tab_spec, tab_spec, tab_spec, tab_spec,
            _layer_vec_spec(l, HEAD_DIM),
            _layer_vec_spec(l, HEAD_DIM),
        ],
        out_specs=pl.BlockSpec((tm, tn), lambda i, j: (i, j)),
        out_shape=jax.ShapeDtypeStruct((t, IN_W), BF16),
        scratch_shapes=[pltpu.VMEM((tm, d), BF16)],
        compiler_params=_params(("arbitrary", "arbitrary")),
        name="inproj",
    )(x2d, g, mod5, mod5, w_in, *tables, qn, kn)


def _window_kernel(q_ref, kp_ref, km_ref, kn_ref, vp_ref, vm_ref, vn_ref, sink_ref, o_ref, s_scr, *,
                   seq):
    qi = pl.program_id(2)
    tq = q_ref.shape[0]
    kcat = jnp.concatenate([kp_ref[...], km_ref[...], kn_ref[...]], axis=0)
    vcat = jnp.concatenate([vp_ref[...], vm_ref[...], vn_ref[...]], axis=0)
    vt = jnp.concatenate([vcat.astype(F32).T.astype(BF16),
                          jnp.ones((DENSE_ONES_ROWS, tq + 2 * WINDOW), BF16)], axis=0)
    sink = sink_ref[...] * LOG2_E
    lanes = GROUP * WINDOW
    r = lax.broadcasted_iota(jnp.int32, (1, lanes), 1) & (WINDOW - 1)
    kk = lax.broadcasted_iota(jnp.int32, (WINDOW, lanes), 0)
    n_sub = tq // WINDOW

    def masked_scores(sb):
        base = qi * tq + (sb - 1) * WINDOW
        lo = jnp.maximum(r, -base)
        hi = jnp.minimum(r + 2 * WINDOW, seq - 1 - base)
        qb = q_ref[sb * WINDOW:(sb + 1) * WINDOW, :]
        qs = jnp.concatenate([qb[:, g * HEAD_DIM:(g + 1) * HEAD_DIM] for g in range(GROUP)], axis=0)
        kj = kcat[sb * WINDOW:(sb + 3) * WINDOW]
        s = lax.dot_general(kj, qs, (((1,), (1,)), ((), ())), preferred_element_type=F32)
        top = jnp.where(kk >= lo, s[:WINDOW], MASK_VALUE)
        mid = s[WINDOW:2 * WINDOW]
        bot = jnp.where(kk + 2 * WINDOW <= hi, s[2 * WINDOW:], MASK_VALUE)
        s_scr[sb % 2, 0:WINDOW] = top
        s_scr[sb % 2, WINDOW:2 * WINDOW] = mid
        s_scr[sb % 2, 2 * WINDOW:] = bot
        col_max = jnp.maximum(jnp.maximum(jnp.max(top, axis=0, keepdims=True),
                                          jnp.max(mid, axis=0, keepdims=True)),
                              jnp.max(bot, axis=0, keepdims=True))
        return jnp.maximum(col_max, sink)

    def finish(sb, m):
        p = jnp.exp2(s_scr[sb % 2] - m).astype(BF16)
        acc = jnp.dot(vt[:, sb * WINDOW:(sb + 3) * WINDOW], p, preferred_element_type=F32)
        denom = acc[HEAD_DIM:HEAD_DIM + 1] + jnp.exp2(sink - m)
        ot = acc[:HEAD_DIM] / denom
        for g in range(GROUP):
            o_ref[sb * WINDOW:(sb + 1) * WINDOW, g * HEAD_DIM:(g + 1) * HEAD_DIM] = (
                ot[:, g * WINDOW:(g + 1) * WINDOW].T.astype(BF16))

    m = masked_scores(0)
    for sb in range(n_sub):
        m_next = masked_scores(sb + 1) if sb + 1 < n_sub else None
        finish(sb, m)
        m = m_next


def _window_attention(z3, sink_rows, l):
    b, seq, _ = z3.shape
    tq = WIN_TQ
    r = tq // WINDOW
    nblk = seq // WINDOW
    gw = CHUNK_W
    heads_per_chunk = CHUNK_W // HEAD_DIM
    k0 = _z_chunk(KA_OFF) * heads_per_chunk
    v0 = k0 + A_KV_HEADS
    q_chunks = [_z_chunk(QA_OFF + g * CHUNK_W) for g in range(A_KV_HEADS)]
    assert q_chunks[1] - q_chunks[0] == 2
    prev = lambda off: (lambda bi, h, qi: (bi, jnp.maximum(qi * r - 1, 0), off + h))
    main = lambda off: (lambda bi, h, qi: (bi, qi, off + h))
    nxt = lambda off: (lambda bi, h, qi: (bi, jnp.minimum((qi + 1) * r, nblk - 1), off + h))
    edge = lambda f: pl.BlockSpec((None, WINDOW, HEAD_DIM), f)
    body = lambda f: pl.BlockSpec((None, tq, HEAD_DIM), f)
    return pl.pallas_call(
        functools.partial(_window_kernel, seq=seq),
        grid=(b, A_KV_HEADS, seq // tq),
        in_specs=[
            pl.BlockSpec((None, tq, gw), lambda bi, h, qi: (bi, qi, q_chunks[0] + 2 * h)),
            edge(prev(k0)), body(main(k0)), edge(nxt(k0)),
            edge(prev(v0)), body(main(v0)), edge(nxt(v0)),
            pl.BlockSpec((None, None, 1, GROUP * WINDOW), lambda bi, h, qi: (l, h, 0, 0)),
        ],
        out_specs=pl.BlockSpec((None, tq, gw), lambda bi, h, qi: (bi, qi, h)),
        out_shape=jax.ShapeDtypeStruct((b, seq, A_Q_HEADS * HEAD_DIM), BF16),
        scratch_shapes=[pltpu.VMEM((2, 3 * WINDOW, GROUP * WINDOW), F32)],
        compiler_params=_params(("arbitrary", "arbitrary", "arbitrary")),
        name="window_attn",
    )(z3, z3, z3, z3, z3, z3, z3, sink_rows)


def _dense_kernel(q_ref, k_ref, v_ref, o_ref, vt_scr, s0_scr, s1_scr):
    tq = q_ref.shape[0]
    seq = k_ref.shape[0]
    cols = GROUP * tq

    @pl.when(pl.program_id(2) == 0)
    def _():
        for c in range(seq // DENSE_TK):
            blk = v_ref[c * DENSE_TK:(c + 1) * DENSE_TK, :].astype(F32)
            vt_scr[:HEAD_DIM, c * DENSE_TK:(c + 1) * DENSE_TK] = blk.T.astype(BF16)
        vt_scr[HEAD_DIM:, :] = jnp.ones((DENSE_ONES_ROWS, seq), BF16)

    q = jnp.concatenate([q_ref[:, g * HEAD_DIM:(g + 1) * HEAD_DIM] for g in range(GROUP)], axis=0)

    def scores(ci, s_scr):
        start = pl.multiple_of(ci * DENSE_TK, DENSE_TK)
        k = k_ref[pl.ds(start, DENSE_TK), :]
        s = lax.dot_general(k, q, (((1,), (1,)), ((), ())), preferred_element_type=F32)
        s_scr[...] = s
        return jnp.max(s, axis=0, keepdims=True)

    def update(ci, s_scr, mc, state):
        m, acc = state
        start = pl.multiple_of(ci * DENSE_TK, DENSE_TK)
        vt = vt_scr[:, pl.ds(start, DENSE_TK)]
        m_new = jnp.maximum(m, mc)
        alpha = jnp.exp2(m - m_new)
        p = jnp.exp2(s_scr[...] - m_new).astype(BF16)
        acc = alpha * acc + jnp.dot(vt, p, preferred_element_type=F32)
        return m_new, acc

    n_chunks = seq // DENSE_TK
    bufs = (s0_scr, s1_scr)
    unroll = DENSE_UNROLL_LONG if n_chunks >= DENSE_LONG_CHUNKS else DENSE_UNROLL_SHORT
    unroll = min(unroll, n_chunks)
    assert n_chunks % unroll == 0 and (unroll % 2 == 0 or n_chunks == 1)

    def group(base, mc, state, last):
        for j in range(unroll):
            c = base + j
            nxt = None if (last and j == unroll - 1) else scores(c + 1, bufs[(j + 1) % 2])
            state = update(c, bufs[j % 2], mc, state)
            mc = nxt
        return mc, state

    state = (jnp.full((1, cols), -jnp.inf, F32),
             jnp.zeros((HEAD_DIM + DENSE_ONES_ROWS, cols), F32))
    mc, state = lax.fori_loop(
        0, n_chunks // unroll - 1,
        lambda i, carry: group(i * unroll, carry[0], carry[1], False),
        (scores(0, bufs[0]), state))
    _, (_, acc) = group(n_chunks - unroll, mc, state, True)
    o = acc[:HEAD_DIM] / acc[HEAD_DIM:HEAD_DIM + 1]
    for g in range(GROUP):
        o_ref[:, g * HEAD_DIM:(g + 1) * HEAD_DIM] = o[:, g * tq:(g + 1) * tq].T.astype(BF16)


def _dense_attention(z3):
    b, seq, _ = z3.shape
    long_seq = seq // DENSE_TK >= DENSE_LONG_CHUNKS
    tq = DENSE_TQ_LONG if long_seq else min(DENSE_TQ_SHORT, seq)
    gw = CHUNK_W
    heads_per_chunk = CHUNK_W // HEAD_DIM
    k0 = _z_chunk(KB_OFF) * heads_per_chunk
    v0 = k0 + B_KV_HEADS
    q_chunks = [_z_chunk(QB_OFF + g * CHUNK_W) for g in range(B_KV_HEADS)]
    assert q_chunks[1] - q_chunks[0] == 2
    return pl.pallas_call(
        _dense_kernel,
        grid=(b, B_KV_HEADS, seq // tq),
        in_specs=[
            pl.BlockSpec((None, tq, gw), lambda bi, h, qi: (bi, qi, q_chunks[0] + 2 * h)),
            pl.BlockSpec((None, seq, HEAD_DIM), lambda bi, h, qi: (bi, 0, k0 + h)),
            pl.BlockSpec((None, seq, HEAD_DIM), lambda bi, h, qi: (bi, 0, v0 + h)),
        ],
        out_specs=pl.BlockSpec((None, tq, gw), lambda bi, h, qi: (bi, qi, h)),
        out_shape=jax.ShapeDtypeStruct((b, seq, B_Q_HEADS * HEAD_DIM), BF16),
        scratch_shapes=[pltpu.VMEM((HEAD_DIM + DENSE_ONES_ROWS, seq), BF16),
                        pltpu.VMEM((DENSE_TK, GROUP * tq), F32),
                        pltpu.VMEM((DENSE_TK, GROUP * tq), F32)],
        compiler_params=_params(("arbitrary", "arbitrary", "arbitrary")),
        name="dense_attn",
    )(z3, z3, z3)


def _postmix_kernel(x_ref, oa_ref, ob_ref, *rest):
    n_gate = D_MODEL // CHUNK_W
    ga_refs, gb_refs = rest[:n_gate], rest[n_gate:2 * n_gate]
    wa_ref, wb_ref, wo_ref, g_ref, gate_ref, o_ref, m_scr = rest[2 * n_gate:]
    for r0 in range(0, x_ref.shape[0], POST_ROWS):
        rows = slice(r0, r0 + POST_ROWS)
        oa = oa_ref[rows, :]
        ob = ob_ref[rows, :]
        for ci, (ga_ref, gb_ref) in enumerate(zip(ga_refs, gb_refs)):
            cs = slice(ci * CHUNK_W, (ci + 1) * CHUNK_W)
            pa = jnp.dot(oa, wa_ref[:, cs], preferred_element_type=F32)
            pb = jnp.dot(ob, wb_ref[:, cs], preferred_element_type=F32)
            merged = ga_ref[rows, :].astype(F32) * pa + gb_ref[rows, :].astype(F32) * pb
            m_scr[rows, cs] = merged.astype(BF16)
        y = jnp.dot(m_scr[rows, :], wo_ref[...], preferred_element_type=F32)
        o_ref[rows, :] = x_ref[rows, :] + gate_ref[...] * _rms(y, g_ref[...])


def _postmix(x2d, oa2d, ob2d, z2d, wa, wb, wo, g, mod5, row0, l, seq):
    t, d = x2d.shape
    tm = POST_TM
    nsb = seq // tm
    n_gate = d // CHUNK_W
    row = lambda i: (i, 0)
    resident = lambda shape: pl.BlockSpec((None,) + tuple(shape[1:]), lambda i: (l, 0, 0),
                                          pipeline_mode=pl.Buffered(1))
    gate_blk = lambda off: pl.BlockSpec((tm, CHUNK_W), functools.partial(
        lambda chunk, i: (i, chunk), _z_chunk(off)))
    gate_specs = ([gate_blk(GA_OFF + c * CHUNK_W) for c in range(n_gate)]
                  + [gate_blk(GB_OFF + c * CHUNK_W) for c in range(n_gate)])
    return pl.pallas_call(
        _postmix_kernel,
        grid=(t // tm,),
        in_specs=[
            pl.BlockSpec((tm, d), row),
            pl.BlockSpec((tm, oa2d.shape[1]), row),
            pl.BlockSpec((tm, ob2d.shape[1]), row),
            *gate_specs,
            resident(wa.shape), resident(wb.shape), resident(wo.shape),
            _layer_vec_spec(l, d),
            _mod_spec(MOD_GATE_M, l, row0, nsb),
        ],
        out_specs=pl.BlockSpec((tm, d), row),
        out_shape=jax.ShapeDtypeStruct((t, d), F32),
        scratch_shapes=[pltpu.VMEM((tm, d), BF16)],
        compiler_params=_params(("arbitrary",)),
        name="postmix",
    )(x2d, oa2d, ob2d, *([z2d] * (2 * n_gate)), wa, wb, wo, g, mod5)


def _ffn_kernel(x_ref, g_ref, sc_ref, sh_ref, w1_ref, w3_ref, w2_ref, gpost_ref, gate_ref, o_ref,
                u_scr):
    k = pl.program_id(1)
    last = pl.num_programs(1) - 1

    def chunk(first, rows=slice(None)):
        u = u_scr[rows, :]
        h1 = jnp.dot(u, w1_ref[...], preferred_element_type=F32)
        h3 = jnp.dot(u, w3_ref[...], preferred_element_type=F32)
        h = (h1 * _sigmoid(h1) * h3).astype(BF16)
        part = jnp.dot(h, w2_ref[...], preferred_element_type=F32)
        if first:
            o_ref[rows, :] = part
        else:
            o_ref[rows, :] += part

    @pl.when(k == 0)
    def _():
        u_scr[...] = _mod_norm(x_ref[...], g_ref[...], sc_ref[...], sh_ref[...]).astype(BF16)
        chunk(True)

    pl.when(jnp.logical_and(k > 0, k < last))(functools.partial(chunk, False))

    @pl.when(k == last)
    def _():
        for r0 in range(0, o_ref.shape[0], FFN_TAIL_ROWS):
            rows = slice(r0, r0 + FFN_TAIL_ROWS)
            chunk(False, rows)
            o_ref[rows, :] = x_ref[rows, :] + gate_ref[...] * _rms(o_ref[rows, :], gpost_ref[...])


def _ffn(x2d, g_pre, mod5, row0, l, w13, w2, g_post, seq):
    t, d = x2d.shape
    tm, tf = FFN_TM, FFN_TF
    d_ff = w2.shape[1]
    nk = d_ff // tf
    nsb = seq // tm
    row = lambda i, k: (i, 0)
    return pl.pallas_call(
        _ffn_kernel,
        grid=(t // tm, nk),
        in_specs=[
            pl.BlockSpec((tm, d), row),
            _layer_vec_spec(l, d),
            _mod_spec(MOD_SCALE_F, l, row0, nsb),
            _mod_spec(MOD_SHIFT_F, l, row0, nsb),
            pl.BlockSpec((None, d, tf), lambda i, k: (l, 0, k)),
            pl.BlockSpec((None, d, tf), lambda i, k: (l, 0, nk + k)),
            pl.BlockSpec((None, tf, d), lambda i, k: (l, k, 0)),
            _layer_vec_spec(l, d),
            _mod_spec(MOD_GATE_F, l, row0, nsb),
        ],
        out_specs=pl.BlockSpec((tm, d), row),
        out_shape=jax.ShapeDtypeStruct((t, d), F32),
        scratch_shapes=[pltpu.VMEM((tm, d), BF16)],
        compiler_params=_params(("arbitrary", "arbitrary")),
        name="ffn",
    )(x2d, g_pre, mod5, mod5, w13, w13, w2, g_post, mod5)


def _rope_tables(seq):
    def angles(pos, dim):
        inv_freq = ROPE_THETA ** (-jnp.arange(0, dim, 2, dtype=F32) / dim)
        ang = pos.astype(F32)[:, None] * inv_freq[None, :]
        return jnp.cos(ang), jnp.sin(ang)

    t = jnp.arange(seq)
    c1, s1 = angles(t, HEAD_DIM)
    cr, sr = angles(t // GRID_W, HEAD_DIM // 2)
    cc, sc = angles(t % GRID_W, HEAD_DIM // 2)
    zero = jnp.zeros_like(sr)
    cos1 = jnp.concatenate([c1, c1], axis=-1)
    sin1 = jnp.concatenate([-s1, s1], axis=-1)
    cosx = jnp.concatenate([cr, cr, cc, cc], axis=-1)
    sin_lo = jnp.concatenate([-sr, zero, -sc, zero], axis=-1)
    sin_hi = jnp.concatenate([zero, sr, zero, sc], axis=-1)
    return cos1, sin1, cosx, sin_lo, sin_hi


def _encoder(x, mod5, row0, w):
    b, seq, d = x.shape
    tables = _rope_tables(seq)
    x2d = x.reshape(b * seq, d)
    for l in range(w["w_in"].shape[0]):
        z2d = _inproj(x2d, mod5, row0, l, w["g_pre_mix"], w["w_in"], tables, w["q_norm_b"],
                      w["k_norm_b"], seq)
        z3 = z2d.reshape(b, seq, IN_W)
        oa = _window_attention(z3, w["sink_rows"], l)
        ob = _dense_attention(z3)
        x2d = _postmix(x2d, oa.reshape(b * seq, -1), ob.reshape(b * seq, -1), z2d, w["w_branch_a"],
                       w["w_branch_b"], w["w_out"], w["g_post_mix"], mod5, row0, l, seq)
        x2d = _ffn(x2d, w["g_pre_ffn"], mod5, row0, l, w["w_13"], w["w_2"], w["g_post_ffn"], seq)
    return x2d.reshape(b, seq, d)


def kernel(x_prompt, x_sample, c_prompt, c_sample, g_pre_mix, g_post_mix, g_pre_ffn, g_post_ffn, w_mod,
           b_mod, w_in, q_norm_b, k_norm_b, sink_a, w_branch_a, w_branch_b, w_out, w_13, w_2):
    depth = w_in.shape[0]
    nb_p, nb_s = c_prompt.shape[0], c_sample.shape[0]
    assert nb_p + nb_s <= MOD_ROWS
    c_all = jnp.zeros((MOD_ROWS, D_MODEL), F32)
    c_all = c_all.at[:nb_p].set(c_prompt).at[nb_p:nb_p + nb_s].set(c_sample)
    mod = _modulation(c_all, w_mod, b_mod)
    mod5 = mod.reshape(depth, MOD_ROWS, N_MOD, 1, D_MODEL)

    vec = lambda a: a.reshape(depth, 1, a.shape[-1])
    weights = {
        "g_pre_mix": vec(g_pre_mix), "g_post_mix": vec(g_post_mix),
        "g_pre_ffn": vec(g_pre_ffn), "g_post_ffn": vec(g_post_ffn),
        "q_norm_b": vec(q_norm_b), "k_norm_b": vec(k_norm_b),
        "sink_rows": jnp.broadcast_to(
            sink_a.astype(F32).reshape(depth, A_KV_HEADS, 1, GROUP, 1),
            (depth, A_KV_HEADS, 1, GROUP, WINDOW)).reshape(depth, A_KV_HEADS, 1, GROUP * WINDOW),
        "w_in": jnp.concatenate([w_in[:, :, off:off + CHUNK_W] for off, _ in Z_CHUNKS],
                                axis=2).astype(BF16),
        "w_branch_a": w_branch_a.astype(BF16), "w_branch_b": w_branch_b.astype(BF16),
        "w_out": w_out.astype(BF16),
        "w_13": w_13.astype(BF16), "w_2": w_2.astype(BF16),
    }

    y_prompt = _encoder(x_prompt, mod5, 0, weights)
    y_sample = _encoder(x_sample, mod5, nb_p, weights)
    return (y_prompt, y_sample)
```

```python
import functools

import jax
import jax.numpy as jnp
from jax import lax
from jax.experimental import pallas as pl
from jax.experimental.pallas import tpu as pltpu

D_MODEL = 2048
HEAD_DIM = 128
A_Q_HEADS = 8
A_KV_HEADS = 2
B_Q_HEADS = 8
B_KV_HEADS = 2
GROUP = A_Q_HEADS // A_KV_HEADS
WINDOW = 128
GRID_W = 64
ROPE_THETA = 10000.0
N_MOD = 6
EPS = 1e-6
MASK_VALUE = -1e30
LOG2_E = 1.4426950408889634

QA_OFF = 0
KA_OFF = QA_OFF + A_Q_HEADS * HEAD_DIM
VA_OFF = KA_OFF + A_KV_HEADS * HEAD_DIM
QB_OFF = VA_OFF + A_KV_HEADS * HEAD_DIM
KB_OFF = QB_OFF + B_Q_HEADS * HEAD_DIM
VB_OFF = KB_OFF + B_KV_HEADS * HEAD_DIM
GA_OFF = VB_OFF + B_KV_HEADS * HEAD_DIM
GB_OFF = GA_OFF + D_MODEL
IN_W = GB_OFF + D_MODEL

CHUNK_W = GROUP * HEAD_DIM
Z_CHUNKS = (
    (QA_OFF, "qa"), (GA_OFF, "gate"),
    (QA_OFF + CHUNK_W, "qa"), (GA_OFF + CHUNK_W, "gate"),
    (KA_OFF, "kva"), (GA_OFF + 2 * CHUNK_W, "gate"),
    (QB_OFF, "qb"), (GA_OFF + 3 * CHUNK_W, "gate"),
    (QB_OFF + CHUNK_W, "qb"), (GB_OFF, "gate"),
    (KB_OFF, "kvb"), (GB_OFF + CHUNK_W, "gate"),
    (GB_OFF + 2 * CHUNK_W, "gate"), (GB_OFF + 3 * CHUNK_W, "gate"),
)
assert len(Z_CHUNKS) * CHUNK_W == IN_W


def _z_chunk(src_off):
    return [off for off, _ in Z_CHUNKS].index(src_off)

V7X_VMEM_BYTES = 64 * 1024 * 1024
VMEM_LIMIT = V7X_VMEM_BYTES - 8 * 1024 * 1024

F32 = jnp.float32
BF16 = jnp.bfloat16

MOD_ROWS = 16
MOD_TN = 1024
INPROJ_TM = 1024
INPROJ_TN = 1024
INPROJ_ROWS = 512
WIN_TQ = 1024
DENSE_TQ_SHORT = 1024
DENSE_TQ_LONG = 256
DENSE_TK = 512
DENSE_ONES_ROWS = 16
DENSE_UNROLL_SHORT = 2
DENSE_UNROLL_LONG = 4
DENSE_LONG_CHUNKS = 16
POST_TM = 512
POST_ROWS = 256
FFN_TM = 1024
FFN_TF = 256
FFN_TAIL_ROWS = 256


def _params(sem):
    return pltpu.CompilerParams(dimension_semantics=sem, vmem_limit_bytes=VMEM_LIMIT)


def _mod_kernel(c_ref, w_ref, b_ref, o_ref):
    c = c_ref[...]
    a = (c * jax.nn.sigmoid(c)).astype(BF16)
    w = w_ref[...].astype(BF16)
    o_ref[...] = jnp.dot(a, w, preferred_element_type=F32) + b_ref[...]


def _modulation(c_all, w_mod, b_mod):
    depth, d, n = w_mod.shape
    return pl.pallas_call(
        _mod_kernel,
        grid=(depth, n // MOD_TN),
        in_specs=[
            pl.BlockSpec((MOD_ROWS, d), lambda l, j: (0, 0)),
            pl.BlockSpec((None, d, MOD_TN), lambda l, j: (l, 0, j)),
            pl.BlockSpec((None, 1, MOD_TN), lambda l, j: (l, 0, j)),
        ],
        out_specs=pl.BlockSpec((None, MOD_ROWS, MOD_TN), lambda l, j: (l, 0, j)),
        out_shape=jax.ShapeDtypeStruct((depth, MOD_ROWS, n), F32),
        compiler_params=_params(("arbitrary", "arbitrary")),
        name="modulation",
    )(c_all, w_mod, b_mod.reshape(depth, 1, n))


def _mod_norm(x, g, scale, shift):
    ms = jnp.mean(x * x, axis=-1, keepdims=True)
    y = x * lax.rsqrt(ms + EPS) * g
    return y * (1.0 + scale) + shift


def _sigmoid(x):
    return 0.5 * jnp.tanh(0.5 * x) + 0.5


def _rms(y, g):
    ms = jnp.mean(y * y, axis=-1, keepdims=True)
    return y * lax.rsqrt(ms + EPS) * g


def _rope1d(h, cos, sin_signed):
    return h * cos + pltpu.roll(h, HEAD_DIM // 2, 1) * sin_signed


def _rope_axial(h, cos, sin_lo, sin_hi):
    q = HEAD_DIM // 4
    return h * cos + pltpu.roll(h, HEAD_DIM - q, 1) * sin_lo + pltpu.roll(h, q, 1) * sin_hi


def _inproj_kernel(x_ref, g_ref, sc_ref, sh_ref, w_ref, cos1_ref, sin1_ref, cosx_ref, sinlo_ref,
                   sinhi_ref, qn_ref, kn_ref, z_ref, u_scr):
    j = pl.program_id(1)
    qscale = HEAD_DIM ** -0.5 * LOG2_E

    def rope_a(v, rows, scale):
        r = _rope1d(v, cos1_ref[rows, :], sin1_ref[rows, :])
        return r * scale if scale != 1.0 else r

    def rope_b(v, rows, gain):
        ms = jnp.mean(v * v, axis=-1, keepdims=True)
        vn = v * lax.rsqrt(ms + EPS) * gain
        return _rope_axial(vn, cosx_ref[rows, :], sinlo_ref[rows, :], sinhi_ref[rows, :])

    q_a = lambda v, rows: rope_a(v, rows, qscale)
    k_a = lambda v, rows: rope_a(v, rows, 1.0)
    q_b = lambda v, rows: rope_b(v, rows, qn_ref[...] * qscale)
    k_b = lambda v, rows: rope_b(v, rows, kn_ref[...])
    keep = lambda v, rows: v
    head_ops = {
        "qa": [q_a] * GROUP,
        "kva": [k_a] * A_KV_HEADS + [keep] * A_KV_HEADS,
        "qb": [q_b] * GROUP,
        "kvb": [k_b] * B_KV_HEADS + [keep] * B_KV_HEADS,
    }
    chunks_per_block = INPROJ_TN // CHUNK_W
    tm = u_scr.shape[0]

    def run(kinds):
        for ci, kind in enumerate(kinds):
            cols = slice(ci * CHUNK_W, (ci + 1) * CHUNK_W)
            for r0 in range(0, tm, INPROJ_ROWS):
                rows = slice(r0, r0 + INPROJ_ROWS)
                acc = jnp.dot(u_scr[rows, :], w_ref[:, cols], preferred_element_type=F32)
                if kind == "gate":
                    z_ref[rows, cols] = _sigmoid(acc).astype(BF16)
                    continue
                for hh, op in enumerate(head_ops[kind]):
                    lo = ci * CHUNK_W + hh * HEAD_DIM
                    v = op(acc[:, hh * HEAD_DIM:(hh + 1) * HEAD_DIM], rows)
                    z_ref[rows, lo:lo + HEAD_DIM] = v.astype(BF16)

    n_blocks = len(Z_CHUNKS) // chunks_per_block
    block_kinds = [tuple(k for _, k in Z_CHUNKS[bj * chunks_per_block:(bj + 1) * chunks_per_block])
                   for bj in range(n_blocks)]
    @pl.when(j == 0)
    def _():
        u_scr[...] = _mod_norm(x_ref[...], g_ref[...], sc_ref[...], sh_ref[...]).astype(BF16)
        run(block_kinds[0])

    for kinds in dict.fromkeys(block_kinds[1:]):
        cond = functools.reduce(jnp.logical_or, [j == bj for bj in range(1, n_blocks)
                                                 if block_kinds[bj] == kinds])
        pl.when(cond)(functools.partial(run, kinds))


MOD_SHIFT_M, MOD_SCALE_M, MOD_GATE_M, MOD_SHIFT_F, MOD_SCALE_F, MOD_GATE_F = range(N_MOD)


def _mod_spec(which, l, row0, rows_per_batch_block):
    return pl.BlockSpec(
        (None, None, None, 1, D_MODEL),
        lambda i, *_: (l, row0 + i // rows_per_batch_block, which, 0, 0))


def _layer_vec_spec(l, width):
    return pl.BlockSpec((None, 1, width), lambda *_: (l, 0, 0))


def _inproj(x2d, mod5, row0, l, g, w_in, tables, qn, kn, seq):
    t, d = x2d.shape
    tm, tn = INPROJ_TM, INPROJ_TN
    nsb = seq // tm
    row = lambda i, j: (i, 0)
    pos = lambda i, j: (i % nsb, 0)
    tab_spec = pl.BlockSpec((tm, HEAD_DIM), pos)
    return pl.pallas_call(
        _inproj_kernel,
        grid=(t // tm, IN_W // tn),
        in_specs=[
            pl.BlockSpec((tm, d), row),
            _layer_vec_spec(l, d),
            _mod_spec(MOD_SCALE_M, l, row0, nsb),
            _mod_spec(MOD_SHIFT_M, l, row0, nsb),
            pl.BlockSpec((None, d, tn), lambda i, j: (l, 0, j)),
            tab_spec, tab_spec, tab_spec, tab_spec, tab_spec,
            _layer_vec_spec(l, HEAD_DIM),
            _layer_vec_spec(l, HEAD_DIM),
        ],
        out_specs=pl.BlockSpec((tm, tn), lambda i, j: (i, j)),
        out_shape=jax.ShapeDtypeStruct((t, IN_W), BF16),
        scratch_shapes=[pltpu.VMEM((tm, d), BF16)],
        compiler_params=_params(("arbitrary", "arbitrary")),
        name="inproj",
    )(x2d, g, mod5, mod5, w_in, *tables, qn, kn)


def _window_kernel(q_ref, kp_ref, km_ref, kn_ref, vp_ref, vm_ref, vn_ref, sink_ref, o_ref, s_scr, *,
                   seq):
    qi = pl.program_id(2)
    tq = q_ref.shape[0]
    kcat = jnp.concatenate([kp_ref[...], km_ref[...], kn_ref[...]], axis=0)
    vcat = jnp.concatenate([vp_ref[...], vm_ref[...], vn_ref[...]], axis=0)
    vt = jnp.concatenate([vcat.astype(F32).T.astype(BF16),
                          jnp.ones((DENSE_ONES_ROWS, tq + 2 * WINDOW), BF16)], axis=0)
    sink = sink_ref[...] * LOG2_E
    lanes = GROUP * WINDOW
    r = lax.broadcasted_iota(jnp.int32, (1, lanes), 1) & (WINDOW - 1)
    kk = lax.broadcasted_iota(jnp.int32, (WINDOW, lanes), 0)
    n_sub = tq // WINDOW

    def masked_scores(sb):
        base = qi * tq + (sb - 1) * WINDOW
        lo = jnp.maximum(r, -base)
        hi = jnp.minimum(r + 2 * WINDOW, seq - 1 - base)
        qb = q_ref[sb * WINDOW:(sb + 1) * WINDOW, :]
        qs = jnp.concatenate([qb[:, g * HEAD_DIM:(g + 1) * HEAD_DIM] for g in range(GROUP)], axis=0)
        kj = kcat[sb * WINDOW:(sb + 3) * WINDOW]
        s = lax.dot_general(kj, qs, (((1,), (1,)), ((), ())), preferred_element_type=F32)
        top = jnp.where(kk >= lo, s[:WINDOW], MASK_VALUE)
        mid = s[WINDOW:2 * WINDOW]
        bot = jnp.where(kk + 2 * WINDOW <= hi, s[2 * WINDOW:], MASK_VALUE)
        s_scr[sb % 2, 0:WINDOW] = top
        s_scr[sb % 2, WINDOW:2 * WINDOW] = mid
        s_scr[sb % 2, 2 * WINDOW:] = bot
        col_max = jnp.maximum(jnp.maximum(jnp.max(top, axis=0, keepdims=True),
                                          jnp.max(mid, axis=0, keepdims=True)),
                              jnp.max(bot, axis=0, keepdims=True))
        return jnp.maximum(col_max, sink)

    def finish(sb, m):
        p = jnp.exp2(s_scr[sb % 2] - m).astype(BF16)
        acc = jnp.dot(vt[:, sb * WINDOW:(sb + 3) * WINDOW], p, preferred_element_type=F32)
        denom = acc[HEAD_DIM:HEAD_DIM + 1] + jnp.exp2(sink - m)
        ot = acc[:HEAD_DIM] / denom
        for g in range(GROUP):
            o_ref[sb * WINDOW:(sb + 1) * WINDOW, g * HEAD_DIM:(g + 1) * HEAD_DIM] = (
                ot[:, g * WINDOW:(g + 1) * WINDOW].T.astype(BF16))

    m = masked_scores(0)
    for sb in range(n_sub):
        m_next = masked_scores(sb + 1) if sb + 1 < n_sub else None
        finish(sb, m)
        m = m_next


def _window_attention(z3, sink_rows, l):
    b, seq, _ = z3.shape
    tq = WIN_TQ
    r = tq // WINDOW
    nblk = seq // WINDOW
    gw = CHUNK_W
    heads_per_chunk = CHUNK_W // HEAD_DIM
    k0 = _z_chunk(KA_OFF) * heads_per_chunk
    v0 = k0 + A_KV_HEADS
    q_chunks = [_z_chunk(QA_OFF + g * CHUNK_W) for g in range(A_KV_HEADS)]
    assert q_chunks[1] - q_chunks[0] == 2
    prev = lambda off: (lambda bi, h, qi: (bi, jnp.maximum(qi * r - 1, 0), off + h))
    main = lambda off: (lambda bi, h, qi: (bi, qi, off + h))
    nxt = lambda off: (lambda bi, h, qi: (bi, jnp.minimum((qi + 1) * r, nblk - 1), off + h))
    edge = lambda f: pl.BlockSpec((None, WINDOW, HEAD_DIM), f)
    body = lambda f: pl.BlockSpec((None, tq, HEAD_DIM), f)
    return pl.pallas_call(
        functools.partial(_window_kernel, seq=seq),
        grid=(b, A_KV_HEADS, seq // tq),
        in_specs=[
            pl.BlockSpec((None, tq, gw), lambda bi, h, qi: (bi, qi, q_chunks[0] + 2 * h)),
            edge(prev(k0)), body(main(k0)), edge(nxt(k0)),
            edge(prev(v0)), body(main(v0)), edge(nxt(v0)),
            pl.BlockSpec((None, None, 1, GROUP * WINDOW), lambda bi, h, qi: (l, h, 0, 0)),
        ],
        out_specs=pl.BlockSpec((None, tq, gw), lambda bi, h, qi: (bi, qi, h)),
        out_shape=jax.ShapeDtypeStruct((b, seq, A_Q_HEADS * HEAD_DIM), BF16),
        scratch_shapes=[pltpu.VMEM((2, 3 * WINDOW, GROUP * WINDOW), F32)],
        compiler_params=_params(("arbitrary", "arbitrary", "arbitrary")),
        name="window_attn",
    )(z3, z3, z3, z3, z3, z3, z3, sink_rows)


def _dense_kernel(q_ref, k_ref, v_ref, o_ref, vt_scr, s0_scr, s1_scr):
    tq = q_ref.shape[0]
    seq = k_ref.shape[0]
    cols = GROUP * tq

    @pl.when(pl.program_id(2) == 0)
    def _():
        for c in range(seq // DENSE_TK):
            blk = v_ref[c * DENSE_TK:(c + 1) * DENSE_TK, :].astype(F32)
            vt_scr[:HEAD_DIM, c * DENSE_TK:(c + 1) * DENSE_TK] = blk.T.astype(BF16)
        vt_scr[HEAD_DIM:, :] = jnp.ones((DENSE_ONES_ROWS, seq), BF16)

    q = jnp.concatenate([q_ref[:, g * HEAD_DIM:(g + 1) * HEAD_DIM] for g in range(GROUP)], axis=0)

    def scores(ci, s_scr):
        start = pl.multiple_of(ci * DENSE_TK, DENSE_TK)
        k = k_ref[pl.ds(start, DENSE_TK), :]
        s = lax.dot_general(k, q, (((1,), (1,)), ((), ())), preferred_element_type=F32)
        s_scr[...] = s
        return jnp.max(s, axis=0, keepdims=True)

    def update(ci, s_scr, mc, state):
        m, acc = state
        start = pl.multiple_of(ci * DENSE_TK, DENSE_TK)
        vt = vt_scr[:, pl.ds(start, DENSE_TK)]
        m_new = jnp.maximum(m, mc)
        alpha = jnp.exp2(m - m_new)
        p = jnp.exp2(s_scr[...] - m_new).astype(BF16)
        acc = alpha * acc + jnp.dot(vt, p, preferred_element_type=F32)
        return m_new, acc

    n_chunks = seq // DENSE_TK
    bufs = (s0_scr, s1_scr)
    unroll = DENSE_UNROLL_LONG if n_chunks >= DENSE_LONG_CHUNKS else DENSE_UNROLL_SHORT
    unroll = min(unroll, n_chunks)
    assert n_chunks % unroll == 0 and (unroll % 2 == 0 or n_chunks == 1)

    def group(base, mc, state, last):
        for j in range(unroll):
            c = base + j
            nxt = None if (last and j == unroll - 1) else scores(c + 1, bufs[(j + 1) % 2])
            state = update(c, bufs[j % 2], mc, state)
            mc = nxt
        return mc, state

    state = (jnp.full((1, cols), -jnp.inf, F32),
             jnp.zeros((HEAD_DIM + DENSE_ONES_ROWS, cols), F32))
    mc, state = lax.fori_loop(
        0, n_chunks // unroll - 1,
        lambda i, carry: group(i * unroll, carry[0], carry[1], False),
        (scores(0, bufs[0]), state))
    _, (_, acc) = group(n_chunks - unroll, mc, state, True)
    o = acc[:HEAD_DIM] / acc[HEAD_DIM:HEAD_DIM + 1]
    for g in range(GROUP):
        o_ref[:, g * HEAD_DIM:(g + 1) * HEAD_DIM] = o[:, g * tq:(g + 1) * tq].T.astype(BF16)


def _dense_attention(z3):
    b, seq, _ = z3.shape
    long_seq = seq // DENSE_TK >= DENSE_LONG_CHUNKS
    tq = DENSE_TQ_LONG if long_seq else min(DENSE_TQ_SHORT, seq)
    gw = CHUNK_W
    heads_per_chunk = CHUNK_W // HEAD_DIM
    k0 = _z_chunk(KB_OFF) * heads_per_chunk
    v0 = k0 + B_KV_HEADS
    q_chunks = [_z_chunk(QB_OFF + g * CHUNK_W) for g in range(B_KV_HEADS)]
    assert q_chunks[1] - q_chunks[0] == 2
    return pl.pallas_call(
        _dense_kernel,
        grid=(b, B_KV_HEADS, seq // tq),
        in_specs=[
            pl.BlockSpec((None, tq, gw), lambda bi, h, qi: (bi, qi, q_chunks[0] + 2 * h)),
            pl.BlockSpec((None, seq, HEAD_DIM), lambda bi, h, qi: (bi, 0, k0 + h)),
            pl.BlockSpec((None, seq, HEAD_DIM), lambda bi, h, qi: (bi, 0, v0 + h)),
        ],
        out_specs=pl.BlockSpec((None, tq, gw), lambda bi, h, qi: (bi, qi, h)),
        out_shape=jax.ShapeDtypeStruct((b, seq, B_Q_HEADS * HEAD_DIM), BF16),
        scratch_shapes=[pltpu.VMEM((HEAD_DIM + DENSE_ONES_ROWS, seq), BF16),
                        pltpu.VMEM((DENSE_TK, GROUP * tq), F32),
                        pltpu.VMEM((DENSE_TK, GROUP * tq), F32)],
        compiler_params=_params(("arbitrary", "arbitrary", "arbitrary")),
        name="dense_attn",
    )(z3, z3, z3)


def _postmix_kernel(x_ref, oa_ref, ob_ref, *rest):
    n_gate = D_MODEL // CHUNK_W
    ga_refs, gb_refs = rest[:n_gate], rest[n_gate:2 * n_gate]
    wa_ref, wb_ref, wo_ref, g_ref, gate_ref, o_ref, m_scr = rest[2 * n_gate:]
    for r0 in range(0, x_ref.shape[0], POST_ROWS):
        rows = slice(r0, r0 + POST_ROWS)
        oa = oa_ref[rows, :]
        ob = ob_ref[rows, :]
        for ci, (ga_ref, gb_ref) in enumerate(zip(ga_refs, gb_refs)):
            cs = slice(ci * CHUNK_W, (ci + 1) * CHUNK_W)
            pa = jnp.dot(oa, wa_ref[:, cs], preferred_element_type=F32)
            pb = jnp.dot(ob, wb_ref[:, cs], preferred_element_type=F32)
            merged = ga_ref[rows, :].astype(F32) * pa + gb_ref[rows, :].astype(F32) * pb
            m_scr[rows, cs] = merged.astype(BF16)
        y = jnp.dot(m_scr[rows, :], wo_ref[...], preferred_element_type=F32)
        o_ref[rows, :] = x_ref[rows, :] + gate_ref[...] * _rms(y, g_ref[...])


def _postmix(x2d, oa2d, ob2d, z2d, wa, wb, wo, g, mod5, row0, l, seq):
    t, d = x2d.shape
    tm = POST_TM
    nsb = seq // tm
    n_gate = d // CHUNK_W
    row = lambda i: (i, 0)
    resident = lambda shape: pl.BlockSpec((None,) + tuple(shape[1:]), lambda i: (l, 0, 0),
                                          pipeline_mode=pl.Buffered(1))
    gate_blk = lambda off: pl.BlockSpec((tm, CHUNK_W), functools.partial(
        lambda chunk, i: (i, chunk), _z_chunk(off)))
    gate_specs = ([gate_blk(GA_OFF + c * CHUNK_W) for c in range(n_gate)]
                  + [gate_blk(GB_OFF + c * CHUNK_W) for c in range(n_gate)])
    return pl.pallas_call(
        _postmix_kernel,
        grid=(t // tm,),
        in_specs=[
            pl.BlockSpec((tm, d), row),
            pl.BlockSpec((tm, oa2d.shape[1]), row),
            pl.BlockSpec((tm, ob2d.shape[1]), row),
            *gate_specs,
            resident(wa.shape), resident(wb.shape), resident(wo.shape),
            _layer_vec_spec(l, d),
            _mod_spec(MOD_GATE_M, l, row0, nsb),
        ],
        out_specs=pl.BlockSpec((tm, d), row),
        out_shape=jax.ShapeDtypeStruct((t, d), F32),
        scratch_shapes=[pltpu.VMEM((tm, d), BF16)],
        compiler_params=_params(("arbitrary",)),
        name="postmix",
    )(x2d, oa2d, ob2d, *([z2d] * (2 * n_gate)), wa, wb, wo, g, mod5)


def _ffn_kernel(x_ref, g_ref, sc_ref, sh_ref, w1_ref, w3_ref, w2_ref, gpost_ref, gate_ref, o_ref,
                u_scr):
    k = pl.program_id(1)
    last = pl.num_programs(1) - 1

    def chunk(first, rows=slice(None)):
        u = u_scr[rows, :]
        h1 = jnp.dot(u, w1_ref[...], preferred_element_type=F32)
        h3 = jnp.dot(u, w3_ref[...], preferred_element_type=F32)
        h = (h1 * _sigmoid(h1) * h3).astype(BF16)
        part = jnp.dot(h, w2_ref[...], preferred_element_type=F32)
        if first:
            o_ref[rows, :] = part
        else:
            o_ref[rows, :] += part

    @pl.when(k == 0)
    def _():
        u_scr[...] = _mod_norm(x_ref[...], g_ref[...], sc_ref[...], sh_ref[...]).astype(BF16)
        chunk(True)

    pl.when(jnp.logical_and(k > 0, k < last))(functools.partial(chunk, False))

    @pl.when(k == last)
    def _():
        for r0 in range(0, o_ref.shape[0], FFN_TAIL_ROWS):
            rows = slice(r0, r0 + FFN_TAIL_ROWS)
            chunk(False, rows)
            o_ref[rows, :] = x_ref[rows, :] + gate_ref[...] * _rms(o_ref[rows, :], gpost_ref[...])


def _ffn(x2d, g_pre, mod5, row0, l, w13, w2, g_post, seq):
    t, d = x2d.shape
    tm, tf = FFN_TM, FFN_TF
    d_ff = w2.shape[1]
    nk = d_ff // tf
    nsb = seq // tm
    row = lambda i, k: (i, 0)
    return pl.pallas_call(
        _ffn_kernel,
        grid=(t // tm, nk),
        in_specs=[
            pl.BlockSpec((tm, d), row),
            _layer_vec_spec(l, d),
            _mod_spec(MOD_SCALE_F, l, row0, nsb),
            _mod_spec(MOD_SHIFT_F, l, row0, nsb),
            pl.BlockSpec((None, d, tf), lambda i, k: (l, 0, k)),
            pl.BlockSpec((None, d, tf), lambda i, k: (l, 0, nk + k)),
            pl.BlockSpec((None, tf, d), lambda i, k: (l, k, 0)),
            _layer_vec_spec(l, d),
            _mod_spec(MOD_GATE_F, l, row0, nsb),
        ],
        out_specs=pl.BlockSpec((tm, d), row),
        out_shape=jax.ShapeDtypeStruct((t, d), F32),
        scratch_shapes=[pltpu.VMEM((tm, d), BF16)],
        compiler_params=_params(("arbitrary", "arbitrary")),
        name="ffn",
    )(x2d, g_pre, mod5, mod5, w13, w13, w2, g_post, mod5)


def _rope_tables(seq):
    def angles(pos, dim):
        inv_freq = ROPE_THETA ** (-jnp.arange(0, dim, 2, dtype=F32) / dim)
        ang = pos.astype(F32)[:, None] * inv_freq[None, :]
        return jnp.cos(ang), jnp.sin(ang)

    t = jnp.arange(seq)
    c1, s1 = angles(t, HEAD_DIM)
    cr, sr = angles(t // GRID_W, HEAD_DIM // 2)
    cc, sc = angles(t % GRID_W, HEAD_DIM // 2)
    zero = jnp.zeros_like(sr)
    cos1 = jnp.concatenate([c1, c1], axis=-1)
    sin1 = jnp.concatenate([-s1, s1], axis=-1)
    cosx = jnp.concatenate([cr, cr, cc, cc], axis=-1)
    sin_lo = jnp.concatenate([-sr, zero, -sc, zero], axis=-1)
    sin_hi = jnp.concatenate([zero, sr, zero, sc], axis=-1)
    return cos1, sin1, cosx, sin_lo, sin_hi


def _encoder(x, mod5, row0, w):
    b, seq, d = x.shape
    tables = _rope_tables(seq)
    x2d = x.reshape(b * seq, d)
    for l in range(w["w_in"].shape[0]):
        z2d = _inproj(x2d, mod5, row0, l, w["g_pre_mix"], w["w_in"], tables, w["q_norm_b"],
                      w["k_norm_b"], seq)
        z3 = z2d.reshape(b, seq, IN_W)
        oa = _window_attention(z3, w["sink_rows"], l)
        ob = _dense_attention(z3)
        x2d = _postmix(x2d, oa.reshape(b * seq, -1), ob.reshape(b * seq, -1), z2d, w["w_branch_a"],
                       w["w_branch_b"], w["w_out"], w["g_post_mix"], mod5, row0, l, seq)
        x2d = _ffn(x2d, w["g_pre_ffn"], mod5, row0, l, w["w_13"], w["w_2"], w["g_post_ffn"], seq)
    return x2d.reshape(b, seq, d)


def kernel(x_prompt, x_sample, c_prompt, c_sample, g_pre_mix, g_post_mix, g_pre_ffn, g_post_ffn, w_mod,
           b_mod, w_in, q_norm_b, k_norm_b, sink_a, w_branch_a, w_branch_b, w_out, w_13, w_2):
    depth = w_in.shape[0]
    nb_p, nb_s = c_prompt.shape[0], c_sample.shape[0]
    assert nb_p + nb_s <= MOD_ROWS
    c_all = jnp.zeros((MOD_ROWS, D_MODEL), F32)
    c_all = c_all.at[:nb_p].set(c_prompt).at[nb_p:nb_p + nb_s].set(c_sample)
    mod = _modulation(c_all, w_mod, b_mod)
    mod5 = mod.reshape(depth, MOD_ROWS, N_MOD, 1, D_MODEL)

    vec = lambda a: a.reshape(depth, 1, a.shape[-1])
    weights = {
        "g_pre_mix": vec(g_pre_mix), "g_post_mix": vec(g_post_mix),
        "g_pre_ffn": vec(g_pre_ffn), "g_post_ffn": vec(g_post_ffn),
        "q_norm_b": vec(q_norm_b), "k_norm_b": vec(k_norm_b),
        "sink_rows": jnp.broadcast_to(
            sink_a.astype(F32).reshape(depth, A_KV_HEADS, 1, GROUP, 1),
            (depth, A_KV_HEADS, 1, GROUP, WINDOW)).reshape(depth, A_KV_HEADS, 1, GROUP * WINDOW),
        "w_in": jnp.concatenate([w_in[:, :, off:off + CHUNK_W] for off, _ in Z_CHUNKS],
                                axis=2).astype(BF16),
        "w_branch_a": w_branch_a.astype(BF16), "w_branch_b": w_branch_b.astype(BF16),
        "w_out": w_out.astype(BF16),
        "w_13": w_13.astype(BF16), "w_2": w_2.astype(BF16),
    }

    y_prompt = _encoder(x_prompt, mod5, 0, weights)
    y_sample = _encoder(x_sample, mod5, nb_p, weights)
    return (y_prompt, y_sample)
```
